```python
import math
import jax, jax.numpy as jnp
from jax import lax
import numpy as np

D_MODEL = 1024
BATCH = 4
SEQ = 8192
DEPTH = 2
DEC_BATCH = 32
DEC_SEQ = 4
PAST_LEN = 16384
PAGE_SIZE = 128

PLE_DIM = 256
DA_HEADS = 4
DA_HDIM = 64
DA_WIDTH = 2 * DA_HEADS * DA_HDIM
HG_HEADS = 4
HG_KDIM = 128
HG_VDIM = 128
HG_WIDTH = HG_HEADS * HG_VDIM
MIX_WIDTH = DA_WIDTH + HG_WIDTH
IN_WIDTH = 3 * DA_WIDTH + 2 * HG_HEADS * HG_KDIM + 2 * HG_WIDTH
D_FF = 4 * D_MODEL
ROPE_THETA = 500000.0
ROT_DIM = DA_HDIM // 4
Q_BLOCK = 128
HG_CHUNK = 64
NORM_EPS = 1e-6
SUBLN_EPS = 1e-5
MASK_VALUE = -1e30

kernel_name = 'hymba_diffattn_hgrn2_step'


def _rmsnorm(x, g, eps=NORM_EPS):
    xf = x.astype(jnp.float32)
    y = xf * lax.rsqrt(jnp.mean(xf * xf, axis=-1, keepdims=True) + eps)
    return (y * g.astype(jnp.float32)).astype(x.dtype)


def _rope(x, pos):
    half = ROT_DIM // 2
    inv = ROPE_THETA ** (-(jnp.arange(half, dtype=jnp.float32) * 2.0) / ROT_DIM)
    ang = pos.astype(jnp.float32)[:, None] * inv[None, :]
    cos = jnp.cos(ang)[:, None, :]
    sin = jnp.sin(ang)[:, None, :]
    x1 = x[..., :half].astype(jnp.float32)
    x2 = x[..., half:ROT_DIM].astype(jnp.float32)
    rot = jnp.concatenate([x1 * cos - x2 * sin, x2 * cos + x1 * sin], axis=-1).astype(x.dtype)
    return jnp.concatenate([rot, x[..., ROT_DIM:]], axis=-1)


def _mixer_inputs(x, norm_g, w_in, pos, lb):
    b, s, _ = x.shape
    z = _rmsnorm(x, norm_g) @ w_in
    o1 = DA_WIDTH
    o2 = 2 * DA_WIDTH
    o3 = 3 * DA_WIDTH
    o4 = o3 + HG_HEADS * HG_KDIM
    o5 = o4 + HG_HEADS * HG_KDIM
    o6 = o5 + HG_WIDTH
    q = _rope(z[..., :o1].reshape(b, s, 2 * DA_HEADS, DA_HDIM), pos)
    k = _rope(z[..., o1:o2].reshape(b, s, 2 * DA_HEADS, DA_HDIM), pos)
    v = z[..., o2:o3].reshape(b, s, DA_HEADS, 2 * DA_HDIM)
    hq = jax.nn.silu(z[..., o3:o4].astype(jnp.float32)).reshape(b, s, HG_HEADS, HG_KDIM)
    zf = z[..., o4:o5].astype(jnp.float32).reshape(b, s, HG_HEADS, HG_KDIM)
    hk = (1.0 - lb) * jax.nn.sigmoid(-zf)
    logf = jnp.log1p(-hk)
    hi = z[..., o5:o6].astype(jnp.float32).reshape(b, s, HG_HEADS, HG_VDIM)
    hg = z[..., o6:].reshape(b, s, HG_HEADS, HG_VDIM)
    return q, k, v, hq, hk, hi, logf, hg


def _diff_weights(scores, lam):
    p = jax.nn.softmax(scores, axis=-1)
    b, _, nq, nk = p.shape
    p = p.reshape(b, DA_HEADS, 2, nq, nk)
    return p[:, :, 0] - lam * p[:, :, 1]


def _diff_attn_prompt(q, k, v, lam):
    b, s = q.shape[:2]
    nb = s // Q_BLOCK
    scale = DA_HDIM ** -0.5
    qb = q.reshape(b, nb, Q_BLOCK, 2 * DA_HEADS, DA_HDIM).swapaxes(0, 1)
    kpos = jnp.arange(s)

    def block(args):
        qblk, start = args
        sc = jnp.einsum('bqjd,bkjd->bjqk', qblk, k, preferred_element_type=jnp.float32) * scale
        qpos = start + jnp.arange(Q_BLOCK)
        sc = jnp.where(kpos[None, :] <= qpos[:, None], sc, MASK_VALUE)
        w = _diff_weights(sc, lam)
        return jnp.einsum('bhqk,bkhe->bqhe', w, v.astype(jnp.float32))

    o = lax.map(block, (qb, jnp.arange(nb) * Q_BLOCK))
    return o.swapaxes(0, 1).reshape(b, s, DA_HEADS, 2 * DA_HDIM)


def _diff_attn_sample(q, k_new, v_new, k_past, v_past, lam):
    t = q.shape[1]
    scale = DA_HDIM ** -0.5
    sp = jnp.einsum('bqjd,bkjd->bjqk', q, k_past, preferred_element_type=jnp.float32) * scale
    sn = jnp.einsum('bqjd,bkjd->bjqk', q, k_new, preferred_element_type=jnp.float32) * scale
    causal = jnp.tril(jnp.ones((t, t), dtype=bool))
    sn = jnp.where(causal, sn, MASK_VALUE)
    w = _diff_weights(jnp.concatenate([sp, sn], axis=-1), lam)
    n_past = k_past.shape[1]
    return (jnp.einsum('bhqk,bkhe->bqhe', w[..., :n_past], v_past.astype(jnp.float32))
            + jnp.einsum('bhqk,bkhe->bqhe', w[..., n_past:], v_new.astype(jnp.float32)))


def _hgrn_chunk(S0, q, k, v, g):
    c = q.shape[1]
    G = jnp.cumsum(g, axis=1)
    causal = jnp.tril(jnp.ones((c, c), dtype=bool))[None, :, :, None, None]
    diff = G[:, :, None] - G[:, None, :]
    decay = jnp.where(causal, jnp.exp(jnp.where(causal, diff, 0.0)), 0.0)
    attn = jnp.einsum('bthk,btshk,bshk->bhts', q, decay, k)
    o = (jnp.einsum('bhts,bshv->bthv', attn, v)
         + jnp.einsum('bthk,bhkv->bthv', q * jnp.exp(G), S0))
    g_last = G[:, -1]
    kd = k * jnp.exp(g_last[:, None] - G)
    S = jnp.exp(g_last)[..., None] * S0 + jnp.einsum('bshk,bshv->bhkv', kd, v)
    return o, S


def _hgrn_prompt(q, k, v, g):
    b, s, h, kd = q.shape
    n = s // HG_CHUNK

    def chunks(a):
        return a.reshape(b, n, HG_CHUNK, h, a.shape[-1]).swapaxes(0, 1)

    def step(S, xs):
        qc, kc, vc, gc = xs
        o, S = _hgrn_chunk(S, qc, kc, vc, gc)
        return S, o

    S0 = jnp.zeros((b, h, kd, HG_VDIM), jnp.float32)
    S, o = lax.scan(step, S0, (chunks(q), chunks(k), chunks(v), chunks(g)))
    return o.swapaxes(0, 1).reshape(b, s, h, HG_VDIM), S


def _mixer_merge(x, ao, ho, hg, lam_init, subln_g, hg_norm_g, w_out):
    b, s, _ = x.shape
    a = _rmsnorm(ao, subln_g, SUBLN_EPS) * (1.0 - lam_init)
    h = _rmsnorm(ho, hg_norm_g) * jax.nn.silu(hg.astype(jnp.float32))
    m = jnp.concatenate([a.reshape(b, s, DA_WIDTH), h.reshape(b, s, HG_WIDTH)], axis=-1)
    return x + m.astype(x.dtype) @ w_out


def _channel_ple(x, pe, norm2_g, w_up, w_down, ple_norm_g, w_gate, w_proj):
    hn = _rmsnorm(x, norm2_g)
    x = x + jnp.square(jax.nn.relu(hn @ w_up)) @ w_down
    gate = jax.nn.sigmoid(_rmsnorm(x, ple_norm_g) @ w_gate)
    return x + gate * (pe @ w_proj)


def setup_inputs(seed: int = 0) -> dict:
    key = jax.random.key(seed)
    ks = jax.random.split(key, 26)
    f32 = jnp.float32
    n_pages = PAST_LEN // PAGE_SIZE
    n_used = DEC_BATCH * n_pages
    n_pool = n_used + max(1, n_used // 4)

    def nrm(k, shape, scale=1.0):
        return jax.random.normal(k, shape, f32) * scale

    def gain(k, shape):
        return 1.0 + 0.02 * jax.random.normal(k, shape, f32)

    page_table = jax.random.permutation(ks[7], n_pool)[:n_used].reshape(DEC_BATCH, n_pages).astype(jnp.int32)
    return {
        'x_prompt': nrm(ks[0], (BATCH, SEQ, D_MODEL)),
        'x_sample': nrm(ks[1], (DEC_BATCH, DEC_SEQ, D_MODEL)),
        'cache_k': nrm(ks[2], (DEPTH, n_pool, PAGE_SIZE, 2 * DA_HEADS, DA_HDIM)),
        'cache_v': nrm(ks[3], (DEPTH, n_pool, PAGE_SIZE, DA_HEADS, 2 * DA_HDIM)),
        'state_h': nrm(ks[4], (DEPTH, DEC_BATCH, HG_HEADS, HG_KDIM, HG_VDIM), 0.5),
        'page_table': page_table,
        'p_prompt': nrm(ks[5], (DEPTH, BATCH, SEQ, PLE_DIM)),
        'p_sample': nrm(ks[6], (DEPTH, DEC_BATCH, DEC_SEQ, PLE_DIM)),
        'norm1_g': gain(ks[8], (DEPTH, D_MODEL)),
        'w_in': nrm(ks[9], (DEPTH, D_MODEL, IN_WIDTH), D_MODEL ** -0.5),
        'lam_q1': nrm(ks[10], (DEPTH, DA_HDIM), 0.1),
        'lam_k1': nrm(ks[11], (DEPTH, DA_HDIM), 0.1),
        'lam_q2': nrm(ks[12], (DEPTH, DA_HDIM), 0.1),
        'lam_k2': nrm(ks[13], (DEPTH, DA_HDIM), 0.1),
        'subln_g': gain(ks[14], (DEPTH, 2 * DA_HDIM)),
        'hg_lb_logits': nrm(ks[15], (DEPTH, HG_HEADS * HG_KDIM), 0.5),
        'hg_norm_g': gain(ks[16], (DEPTH, HG_VDIM)),
        'w_out': nrm(ks[17], (DEPTH, MIX_WIDTH, D_MODEL), MIX_WIDTH ** -0.5),
        'norm2_g': gain(ks[18], (DEPTH, D_MODEL)),
        'w_up': nrm(ks[19], (DEPTH, D_MODEL, D_FF), D_MODEL ** -0.5),
        'w_down': nrm(ks[20], (DEPTH, D_FF, D_MODEL), D_FF ** -0.5),
        'ple_norm_g': gain(ks[21], (DEPTH, D_MODEL)),
        'w_ple_gate': nrm(ks[22], (DEPTH, D_MODEL, D_MODEL), D_MODEL ** -0.5),
        'w_ple_proj': nrm(ks[23], (DEPTH, PLE_DIM, D_MODEL), PLE_DIM ** -0.5),
        'final_norm_g': gain(ks[24], (D_MODEL,)),
    }


def reference(x_prompt, x_sample, cache_k, cache_v, state_h, page_table, p_prompt, p_sample,
              norm1_g, w_in, lam_q1, lam_k1, lam_q2, lam_k2, subln_g, hg_lb_logits, hg_norm_g,
              w_out, norm2_g, w_up, w_down, ple_norm_g, w_ple_gate, w_ple_proj, final_norm_g):
    f32 = jnp.float32
    n_sample, t_new = x_sample.shape[:2]
    n_pages = page_table.shape[1]
    pos_p = jnp.arange(x_prompt.shape[1])
    pos_s = PAST_LEN + jnp.arange(t_new)
    sm = jax.nn.softmax(hg_lb_logits.astype(f32), axis=0)
    lbs = (jnp.cumsum(sm, axis=0) - sm[:1]).reshape(DEPTH, HG_HEADS, HG_KDIM)

    xp, xs = x_prompt, x_sample
    kp_l, vp_l, hp_l, ks_l, vs_l, hs_l = [], [], [], [], [], []
    for i in range(DEPTH):
        lam_init = 0.8 - 0.6 * math.exp(-0.3 * i)
        lam = (jnp.exp(jnp.sum((lam_q1[i] * lam_k1[i]).astype(f32)))
               - jnp.exp(jnp.sum((lam_q2[i] * lam_k2[i]).astype(f32))) + lam_init)

        q, k, v, hq, hk, hi, hlf, hg = _mixer_inputs(xp, norm1_g[i], w_in[i], pos_p, lbs[i])
        ao = _diff_attn_prompt(q, k, v, lam)
        ho, hS = _hgrn_prompt(hq, hk, hi, hlf)
        xp = _mixer_merge(xp, ao, ho, hg, lam_init, subln_g[i], hg_norm_g[i], w_out[i])
        xp = _channel_ple(xp, p_prompt[i], norm2_g[i], w_up[i], w_down[i],
                          ple_norm_g[i], w_ple_gate[i], w_ple_proj[i])
        kp_l.append(k)
        vp_l.append(v)
        hp_l.append(hS)

        q, k, v, hq, hk, hi, hlf, hg = _mixer_inputs(xs, norm1_g[i], w_in[i], pos_s, lbs[i])
        k_past = cache_k[i][page_table].reshape(n_sample, n_pages * PAGE_SIZE, 2 * DA_HEADS, DA_HDIM)
        v_past = cache_v[i][page_table].reshape(n_sample, n_pages * PAGE_SIZE, DA_HEADS, 2 * DA_HDIM)
        ao = _diff_attn_sample(q, k, v, k_past, v_past, lam)
        ho, hS = _hgrn_chunk(state_h[i].astype(f32), hq, hk, hi, hlf)
        xs = _mixer_merge(xs, ao, ho, hg, lam_init, subln_g[i], hg_norm_g[i], w_out[i])
        xs = _channel_ple(xs, p_sample[i], norm2_g[i], w_up[i], w_down[i],
                          ple_norm_g[i], w_ple_gate[i], w_ple_proj[i])
        ks_l.append(k)
        vs_l.append(v)
        hs_l.append(hS)

    y_prompt = _rmsnorm(xp, final_norm_g)
    y_sample = _rmsnorm(xs, final_norm_g)
    k_prompt = jnp.stack(kp_l)
    v_prompt = jnp.stack(vp_l)
    h_prompt = jnp.stack(hp_l)
    k_sample = jnp.stack(ks_l)
    v_sample = jnp.stack(vs_l)
    h_sample = jnp.stack(hs_l)
    return (y_prompt, y_sample, k_prompt, v_prompt, h_prompt, k_sample, v_sample, h_sample)
```

```python
import functools
import math

import numpy as np
import jax
import jax.numpy as jnp
from jax import lax
from jax.experimental import pallas as pl
from jax.experimental.pallas import tpu as pltpu

F32 = jnp.float32
BF16 = jnp.bfloat16

DA_HEADS = 4
DA_HDIM = 64
DA_WIDTH = 2 * DA_HEADS * DA_HDIM
HG_HEADS = 4
HG_DIM = 128
HG_WIDTH = HG_HEADS * HG_DIM
N_GROUPS = 7
ROPE_THETA = 500000.0
ROT_DIM = DA_HDIM // 4
ROT_HALF = ROT_DIM // 2
NORM_EPS = 1e-6
SUBLN_EPS = 1e-5
MASK_VALUE = -1e30
LANES = 128
VMEM_LIMIT = 56 * 1024 * 1024

HG_CHUNK = 64
HG_CHUNK_SAMPLE = 16
ATTN_TQ = 256
ATTN_TK = 512
PAGES_PER_STEP = 8
SAMPLE_TPAD = 8


def _cparams(sem):
    return pltpu.CompilerParams(dimension_semantics=sem, vmem_limit_bytes=VMEM_LIMIT)


def _dot(a, b):
    return jnp.dot(a, b, preferred_element_type=F32)


def _dot_nt(a, b):
    return lax.dot_general(a, b, (((1,), (1,)), ((), ())), preferred_element_type=F32)


def _dot_tn(a, b):
    return lax.dot_general(a, b, (((0,), (0,)), ((), ())), preferred_element_type=F32)


def _rms(x, eps):
    return x * lax.rsqrt(jnp.mean(x * x, axis=-1, keepdims=True) + eps)


def _sigmoid(x):
    return 1.0 / (1.0 + jnp.exp(-x))


def _rope_rows(z, c_ref, s1_ref, s2_ref):
    c, s1, s2 = c_ref[...], s1_ref[...], s2_ref[...]
    outs = []
    for j in range(DA_WIDTH // LANES):
        zc = z[:, j * LANES:(j + 1) * LANES]
        up = pltpu.roll(zc, LANES - ROT_HALF, 1)
        dn = pltpu.roll(zc, ROT_HALF, 1)
        outs.append(zc * c + up * s1 + dn * s2)
    return jnp.concatenate(outs, axis=1)


def _proj_in_kernel(*refs, layer, k_transposed):
    if k_transposed:
        (x_ref, g_ref, w_ref, wkt_ref, c_ref, s1_ref, s2_ref, ct_ref, st_ref, lbl_ref,
         _kt_alias, _v_alias,
         q_ref, kt_ref, ktb_ref, v_ref, vb_ref, hq_ref, hk_ref, lg_ref, hi_ref, hg_ref) = refs
    else:
        (x_ref, g_ref, w_ref, c_ref, s1_ref, s2_ref, lbl_ref,
         q_ref, k_ref, kb_ref, v_ref, vb_ref, hq_ref, hk_ref, lg_ref, hi_ref, hg_ref) = refs
    xb = (_rms(x_ref[...], NORM_EPS) * g_ref[...]).astype(BF16)

    def cols(g):
        return _dot(xb, w_ref[:, g * DA_WIDTH:(g + 1) * DA_WIDTH])

    q_ref[...] = (_rope_rows(cols(0), c_ref, s1_ref, s2_ref) * (DA_HDIM ** -0.5)).astype(BF16)
    if k_transposed:
        zk = _dot_nt(wkt_ref[...], xb)
        kt_ref[...] = zk
        ktb_ref[...] = zk.astype(BF16)
        ct, st = ct_ref[...], st_ref[...]
        for j in range(2 * DA_HEADS):
            r = j * DA_HDIM
            x1 = zk[r:r + ROT_HALF]
            x2 = zk[r + ROT_HALF:r + ROT_DIM]
            o1 = x1 * ct - x2 * st
            o2 = x2 * ct + x1 * st
            kt_ref[r:r + ROT_HALF, :] = o1
            kt_ref[r + ROT_HALF:r + ROT_DIM, :] = o2
            ktb_ref[r:r + ROT_DIM, :] = jnp.concatenate([o1, o2], axis=0).astype(BF16)
    else:
        k = _rope_rows(cols(1), c_ref, s1_ref, s2_ref)
        k_ref[...] = k
        kb_ref[...] = k.astype(BF16)
    v = cols(2)
    v_ref[...] = v
    vb_ref[...] = v.astype(BF16)
    zq = cols(3)
    hq_ref[...] = zq * _sigmoid(zq)
    lbl = lbl_ref[...]
    e = jnp.exp(lbl - jnp.max(lbl, axis=0, keepdims=True))
    sm = e / jnp.sum(e, axis=0, keepdims=True)
    lb = jnp.zeros_like(sm[0:1])
    for l in range(1, layer + 1):
        lb = lb + sm[l:l + 1]
    hk = (1.0 - lb) * _sigmoid(-cols(4))
    hk_ref[...] = hk
    lg_ref[...] = jnp.log1p(-hk)
    hi_ref[...] = cols(5)
    hg_ref[...] = cols(6)


def _rope_tables(pos):
    inv = ROPE_THETA ** (-(jnp.arange(ROT_HALF, dtype=F32) * 2.0) / ROT_DIM)
    ang = pos.astype(F32)[:, None] * inv[None, :]
    cos, sin = jnp.cos(ang), jnp.sin(ang)
    d = np.arange(LANES) % DA_HDIM
    f = d % ROT_HALF
    cosl, sinl = cos[:, f], sin[:, f]
    c = jnp.where((d < ROT_DIM)[None, :], cosl, 1.0)
    s1 = jnp.where((d < ROT_HALF)[None, :], -sinl, 0.0)
    s2 = jnp.where(((d >= ROT_HALF) & (d < ROT_DIM))[None, :], sinl, 0.0)
    return c, s1, s2, cos.T, sin.T


def _proj_in(x2d, g_row, w_bf, wkt_bf, tables, lbl, layer, tm, period, kt_prev=None, v_prev=None,
             depth=None, k_transposed=True):
    n, d = x2d.shape
    npt = period // tm
    nb = n // period
    c, s1, s2, ct, st = tables
    row = lambda i: (i, 0)
    const = lambda i: (0, 0)
    tab = lambda i: (i % npt, 0)
    in_specs = [pl.BlockSpec((tm, d), row), pl.BlockSpec((1, d), const),
                pl.BlockSpec(w_bf.shape, const)]
    args = [x2d, g_row, w_bf]
    if k_transposed:
        in_specs.append(pl.BlockSpec(wkt_bf.shape, const))
        args.append(wkt_bf)
    in_specs += [pl.BlockSpec((tm, LANES), tab)] * 3
    args += [c, s1, s2]
    if k_transposed:
        in_specs += [pl.BlockSpec((ROT_HALF, tm), lambda i: (0, i % npt))] * 2
        args += [ct, st]
    in_specs.append(pl.BlockSpec(lbl.shape, const))
    args.append(lbl)
    act = jax.ShapeDtypeStruct((n, DA_WIDTH), F32)
    act_b = jax.ShapeDtypeStruct((n, DA_WIDTH), BF16)
    act_spec = pl.BlockSpec((tm, DA_WIDTH), row)
    aliases = {}
    if k_transposed:
        kt_shape = jax.ShapeDtypeStruct((depth, nb, DA_WIDTH, period), F32)
        v_shape = jax.ShapeDtypeStruct((depth, n, DA_WIDTH), F32)
        if kt_prev is not None:
            aliases = {len(args): 1, len(args) + 1: 3}
            in_specs += [pl.BlockSpec(memory_space=pl.ANY)] * 2
            args += [kt_prev, v_prev]
        else:
            in_specs += [pl.BlockSpec((1, d), const)] * 2
            args += [g_row, g_row]
        out_shape = [act_b, kt_shape, jax.ShapeDtypeStruct((nb, DA_WIDTH, period), BF16),
                     v_shape, act_b] + [act] * 5
        out_specs = [act_spec,
                     pl.BlockSpec((None, None, DA_WIDTH, tm), lambda i: (layer, i // npt, 0, i % npt)),
                     pl.BlockSpec((None, DA_WIDTH, tm), lambda i: (i // npt, 0, i % npt)),
                     pl.BlockSpec((None, tm, DA_WIDTH), lambda i: (layer, i, 0)),
                     act_spec] + [act_spec] * 5
    else:
        out_shape = [act_b, act, act_b, act, act_b] + [act] * 5
        out_specs = [act_spec] * 10
    return pl.pallas_call(
        functools.partial(_proj_in_kernel, layer=layer, k_transposed=k_transposed),
        out_shape=out_shape, grid=(n // tm,), in_specs=in_specs, out_specs=out_specs,
        input_output_aliases=aliases, compiler_params=_cparams(("arbitrary",)),
        name="proj_in_prompt" if k_transposed else "proj_in_sample")(*args)


def _lam_value(lq1, lk1, lq2, lk2, lam_init):
    a = jnp.sum(lq1[...] * lk1[...], axis=-1, keepdims=True)
    b = jnp.sum(lq2[...] * lk2[...], axis=-1, keepdims=True)
    return jnp.exp(a) - jnp.exp(b) + lam_init


def _attn_prompt_kernel(q_ref, kt_ref, v_ref, lq1, lk1, lq2, lk2, o_ref, qq_ref, m_ref, acc_ref,
                        *, tq, tk, lam_init):
    qi = pl.program_id(2)
    q = q_ref[...]
    lane = lax.broadcasted_iota(jnp.int32, q.shape, 1)
    zero = jnp.zeros_like(q)
    qq_ref[0:tq, :] = jnp.where(lane < DA_HDIM, q, zero)
    qq_ref[tq:2 * tq, :] = jnp.where(lane >= DA_HDIM, q, zero)
    m_ref[...] = jnp.full(m_ref.shape, MASK_VALUE, F32)
    acc_ref[...] = jnp.zeros(acc_ref.shape, F32)
    ones = jnp.ones((tk, LANES), BF16)

    def block(kb, masked):
        off = pl.multiple_of(kb * tk, tk)
        s = _dot(qq_ref[...], kt_ref[:, pl.ds(off, tk)])
        if masked:
            row = qi * tq + lax.broadcasted_iota(jnp.int32, s.shape, 0) % tq
            col = off + lax.broadcasted_iota(jnp.int32, s.shape, 1)
            s = jnp.where(col <= row, s, MASK_VALUE)
        m_prev = m_ref[...]
        m_next = jnp.maximum(m_prev, jnp.max(s, axis=1, keepdims=True))
        alpha = jnp.exp(m_prev - m_next)
        p = jnp.exp(s - jnp.concatenate([m_next] * (tk // LANES), axis=1)).astype(BF16)
        vx = jnp.concatenate([v_ref[pl.ds(off, tk), :], ones], axis=1)
        acc_ref[...] = acc_ref[...] * jnp.concatenate([alpha, alpha], axis=1) + _dot(p, vx)
        m_ref[...] = m_next

    nfull = (qi * tq) // tk

    def body(kb, carry):
        block(kb, False)
        return carry

    lax.fori_loop(0, nfull, body, 0)
    block(nfull, True)
    acc = acc_ref[...]
    o1 = acc[0:tq, 0:LANES] / acc[0:tq, LANES:]
    o2 = acc[tq:, 0:LANES] / acc[tq:, LANES:]
    o_ref[...] = o1 - _lam_value(lq1, lk1, lq2, lk2, lam_init) * o2


def _attn_prompt(q_bf, kt_bf, v_bf, lams, lam_init, nb, seq):
    tq, tk = min(ATTN_TQ, seq), min(ATTN_TK, seq)
    nq = seq // tq
    lam_spec = pl.BlockSpec((1, DA_HDIM), lambda b, h, i: (0, 0))
    return pl.pallas_call(
        functools.partial(_attn_prompt_kernel, tq=tq, tk=tk, lam_init=lam_init),
        out_shape=jax.ShapeDtypeStruct(q_bf.shape, F32),
        grid=(nb, DA_HEADS, nq),
        in_specs=[pl.BlockSpec((tq, LANES), lambda b, h, i: (b * nq + i, h)),
                  pl.BlockSpec((None, LANES, seq), lambda b, h, i: (b, h, 0)),
                  pl.BlockSpec((seq, LANES), lambda b, h, i: (b, h))] + [lam_spec] * 4,
        out_specs=pl.BlockSpec((tq, LANES), lambda b, h, i: (b * nq + i, h)),
        scratch_shapes=[pltpu.VMEM((2 * tq, LANES), BF16), pltpu.VMEM((2 * tq, LANES), F32),
                        pltpu.VMEM((2 * tq, 2 * LANES), F32)],
        compiler_params=_cparams(("arbitrary",) * 3), name="attn_prompt")(q_bf, kt_bf, v_bf, *lams)


def _hgrn_consts(c):
    nl = int(math.log2(c))
    t = np.arange(c)
    dq = np.zeros((nl, c, c), np.float32)
    dk = np.zeros((nl, c, c), np.float32)
    msk = np.zeros((nl + 1, c, c), np.float32)
    for l in range(nl):
        b = 1 << l
        mid = (t // (2 * b)) * 2 * b + b
        upper = t >= mid
        for i in range(c):
            if upper[i]:
                dq[l, i, mid[i]:i + 1] = 1.0
            else:
                dk[l, i, i + 1:mid[i]] = 1.0
        same = (t[:, None] // (2 * b)) == (t[None, :] // (2 * b))
        msk[l] = same & upper[:, None] & (~upper)[None, :]
    msk[nl] = np.eye(c)
    tri = np.tril(np.ones((c, c), np.float32))
    rest = np.triu(np.ones((c, c), np.float32), 1)
    d = np.concatenate([dq.reshape(nl * c, c), dk.reshape(nl * c, c), tri, rest], axis=0)
    return jnp.asarray(d, BF16), jnp.asarray(msk, F32), nl


def _hgrn_kernel(*refs, c, nl, nc, has_s0):
    if has_s0:
        hq_ref, hk_ref, hv_ref, lg_ref, d_ref, msk_ref, s0_ref, o_ref, sout_ref, st_ref = refs
    else:
        hq_ref, hk_ref, hv_ref, lg_ref, d_ref, msk_ref, o_ref, sout_ref, st_ref = refs
    jb = pl.program_id(1)

    @pl.when(jb == 0)
    def _():
        for h in range(HG_HEADS):
            st_ref[h] = s0_ref[h].T if has_s0 else jnp.zeros((HG_DIM, HG_DIM), F32)

    def chunk(ci, carry):
        r0 = pl.multiple_of(ci * c, c)
        for h in range(HG_HEADS):
            sl = (pl.ds(r0, c), slice(h * HG_DIM, (h + 1) * HG_DIM))
            q, k, v, g = hq_ref[sl], hk_ref[sl], hv_ref[sl], lg_ref[sl]
            g_hi = g.astype(BF16)
            g_lo = (g - g_hi.astype(F32)).astype(BF16)
            e2 = _dot(d_ref[...], jnp.concatenate([g_hi, g_lo], axis=1))
            e = e2[:, 0:HG_DIM] + e2[:, HG_DIM:]
            qb, kb, vb = q.astype(BF16), k.astype(BF16), v.astype(BF16)
            a = _dot_nt(qb, kb) * msk_ref[nl]
            for l in range(nl):
                qt = (q * jnp.exp(e[l * c:(l + 1) * c])).astype(BF16)
                kt = (k * jnp.exp(e[(nl + l) * c:(nl + l + 1) * c])).astype(BF16)
                a = a + _dot_nt(qt, kt) * msk_ref[l]
            gcum = e[2 * nl * c:(2 * nl + 1) * c]
            grest = e[(2 * nl + 1) * c:(2 * nl + 2) * c]
            st = st_ref[h]
            o = _dot(a.astype(BF16), vb) + _dot_nt((q * jnp.exp(gcum)).astype(BF16), st.astype(BF16))
            o_ref[sl] = o
            khat = (k * jnp.exp(grest)).astype(BF16)
            st_ref[h] = st * jnp.exp(gcum[c - 1:c, :]) + _dot_tn(vb, khat)
        return carry

    lax.fori_loop(0, nc, chunk, 0)

    @pl.when(jb == pl.num_programs(1) - 1)
    def _():
        for h in range(HG_HEADS):
            sout_ref[h] = st_ref[h].T


def _hgrn(hq, hk, hv, lg, s0, nb, seq, c, tb):
    d, msk, nl = _hgrn_consts(c)
    nblk = seq // tb
    row = pl.BlockSpec((tb, HG_WIDTH), lambda b, j: (b * nblk + j, 0))
    st_spec = pl.BlockSpec((None, HG_HEADS, HG_DIM, HG_DIM), lambda b, j: (b, 0, 0, 0))
    in_specs = [row] * 4 + [pl.BlockSpec(d.shape, lambda b, j: (0, 0)),
                            pl.BlockSpec(msk.shape, lambda b, j: (0, 0, 0))]
    args = [hq, hk, hv, lg, d, msk]
    if s0 is not None:
        in_specs.append(st_spec)
        args.append(s0)
    return pl.pallas_call(
        functools.partial(_hgrn_kernel, c=c, nl=nl, nc=tb // c, has_s0=s0 is not None),
        out_shape=[jax.ShapeDtypeStruct(hq.shape, F32),
                   jax.ShapeDtypeStruct((nb, HG_HEADS, HG_DIM, HG_DIM), F32)],
        grid=(nb, nblk), in_specs=in_specs, out_specs=[row, st_spec],
        scratch_shapes=[pltpu.VMEM((HG_HEADS, HG_DIM, HG_DIM), F32)],
        compiler_params=_cparams(("arbitrary", "arbitrary")),
        name="hgrn_sample" if s0 is not None else "hgrn_prompt")(*args)


def _merge_kernel(ao_ref, ho_ref, hg_ref, x_ref, sg_ref, ng_ref, w_ref, o_ref, *, lam_init):
    parts = []
    for h in range(DA_HEADS):
        a = ao_ref[:, h * LANES:(h + 1) * LANES]
        parts.append(_rms(a, SUBLN_EPS) * sg_ref[...] * (1.0 - lam_init))
    for h in range(HG_HEADS):
        sl = slice(h * HG_DIM, (h + 1) * HG_DIM)
        gate = hg_ref[:, sl]
        parts.append(_rms(ho_ref[:, sl], NORM_EPS) * ng_ref[...] * (gate * _sigmoid(gate)))
    m = jnp.concatenate(parts, axis=1).astype(BF16)
    o_ref[...] = x_ref[...] + _dot(m, w_ref[...])


def _merge(ao, ho, hg, x2d, sg_row, ng_row, w_bf, lam_init, tm):
    n, d = x2d.shape
    row = lambda i: (i, 0)
    const = lambda i: (0, 0)
    half = pl.BlockSpec((tm, DA_WIDTH), row)
    return pl.pallas_call(
        functools.partial(_merge_kernel, lam_init=lam_init),
        out_shape=jax.ShapeDtypeStruct((n, d), F32), grid=(n // tm,),
        in_specs=[half, half, half, pl.BlockSpec((tm, d), row),
                  pl.BlockSpec((1, LANES), const), pl.BlockSpec((1, LANES), const),
                  pl.BlockSpec(w_bf.shape, const)],
        out_specs=pl.BlockSpec((tm, d), row),
        compiler_params=_cparams(("arbitrary",)), name="merge_out")(ao, ho, hg, x2d, sg_row, ng_row, w_bf)


def _mlp_kernel(*refs, ff_chunk, last):
    (x_ref, pe_ref, n2_ref, wu_ref, wd_ref, pn_ref, wg_ref, wp_ref) = refs[:8]
    x = x_ref[...]
    hn = (_rms(x, NORM_EPS) * n2_ref[...]).astype(BF16)
    acc = x
    for c0 in range(0, wu_ref.shape[1], ff_chunk):
        u = jnp.maximum(_dot(hn, wu_ref[:, c0:c0 + ff_chunk]), 0.0)
        acc = acc + _dot((u * u).astype(BF16), wd_ref[c0:c0 + ff_chunk, :])
    gate = _sigmoid(_dot((_rms(acc, NORM_EPS) * pn_ref[...]).astype(BF16), wg_ref[...]))
    y = acc + gate * _dot(pe_ref[...].astype(BF16), wp_ref[...])
    if last:
        fn_ref, o_ref = refs[8:]
        o_ref[...] = _rms(y, NORM_EPS) * fn_ref[...]
    else:
        refs[8][...] = y


def _mlp(x2d, pe, n2_row, wu_bf, wd_bf, pn_row, wg_bf, wp_bf, fn_row, tm):
    n, d = x2d.shape
    row = lambda i: (i, 0)
    const = lambda i: (0, 0)
    vec = pl.BlockSpec((1, d), const)
    in_specs = [pl.BlockSpec((tm, d), row), pl.BlockSpec((tm, pe.shape[1]), row), vec,
                pl.BlockSpec(wu_bf.shape, const), pl.BlockSpec(wd_bf.shape, const), vec,
                pl.BlockSpec(wg_bf.shape, const), pl.BlockSpec(wp_bf.shape, const)]
    args = [x2d, pe, n2_row, wu_bf, wd_bf, pn_row, wg_bf, wp_bf]
    if fn_row is not None:
        in_specs.append(vec)
        args.append(fn_row)
    return pl.pallas_call(
        functools.partial(_mlp_kernel, ff_chunk=d, last=fn_row is not None),
        out_shape=jax.ShapeDtypeStruct((n, d), F32), grid=(n // tm,),
        in_specs=in_specs, out_specs=pl.BlockSpec((tm, d), row),
        compiler_params=_cparams(("arbitrary",)), name="mlp_ple")(*args)


def _attn_sample_kernel(pt_ref, qx_ref, ktn_ref, vn_ref, *refs, pp, page, lam_init):
    kt_refs = refs[:pp]
    v_refs = refs[pp:2 * pp]
    lq1, lk1, lq2, lk2, o_ref, m_ref, l_ref, acc_ref = refs[2 * pp:]
    j = pl.program_id(1)
    rows = qx_ref.shape[0]
    qx = qx_ref[...]
    rhead = (lax.broadcasted_iota(jnp.int32, (rows, LANES), 0) // SAMPLE_TPAD) % DA_HEADS

    def update(kts, vs, mask):
        s = jnp.concatenate([_dot(qx, kt[...].astype(BF16)) for kt in kts], axis=1)
        if mask is not None:
            s = jnp.where(mask, s, MASK_VALUE)
        m_prev = m_ref[...]
        m_next = jnp.maximum(m_prev, jnp.max(s, axis=1, keepdims=True))
        alpha = jnp.exp(m_prev - m_next)
        p = jnp.exp(s - jnp.concatenate([m_next] * len(kts), axis=1))
        l_ref[...] = l_ref[...] * alpha + jnp.sum(p, axis=1, keepdims=True)
        pb = p.astype(BF16)
        acc = acc_ref[...] * alpha
        for i, v_ref in enumerate(vs):
            pi = pb[:, i * page:(i + 1) * page]
            for h in range(DA_HEADS):
                vh = v_ref[pl.ds(h, page, stride=DA_HEADS), :].astype(BF16)
                acc = acc + jnp.where(rhead == h, _dot(pi, vh), 0.0)
        acc_ref[...] = acc
        m_ref[...] = m_next

    @pl.when(j == 0)
    def _():
        m_ref[...] = jnp.full(m_ref.shape, MASK_VALUE, F32)
        l_ref[...] = jnp.zeros(l_ref.shape, F32)
        acc_ref[...] = jnp.zeros(acc_ref.shape, F32)
        tok = lax.broadcasted_iota(jnp.int32, (rows, page), 0) % SAMPLE_TPAD
        col = lax.broadcasted_iota(jnp.int32, (rows, page), 1)
        update([ktn_ref], [vn_ref], col <= tok)

    update(kt_refs, v_refs, None)

    @pl.when(j == pl.num_programs(1) - 1)
    def _():
        o = acc_ref[...] / l_ref[...]
        lam = _lam_value(lq1, lk1, lq2, lk2, lam_init)
        half = rows // 2
        for h in range(DA_HEADS):
            pos = o[h * SAMPLE_TPAD:(h + 1) * SAMPLE_TPAD]
            neg = o[half + h * SAMPLE_TPAD:half + (h + 1) * SAMPLE_TPAD]
            o_ref[:, h * LANES:(h + 1) * LANES] = pos - lam * neg


def _attn_sample(page_table, qx, ktn, vn, cache_kt, cache_v2, lams, lam_init, layer):
    nbatch, n_pages = page_table.shape
    page = cache_kt.shape[-1]
    pp = PAGES_PER_STEP
    while n_pages % pp:
        pp //= 2
    rows = qx.shape[1]
    pt_flat = page_table.reshape(-1)

    def page_map(i):
        return lambda b, j, pt: (layer, pt[b * n_pages + j * pp + i], 0, 0)

    per_b = lambda b, j, pt: (b, 0, 0)
    lam_spec = pl.BlockSpec((1, DA_HDIM), lambda b, j, pt: (0, 0))
    kt_specs = [pl.BlockSpec((None, None, DA_WIDTH, page), page_map(i)) for i in range(pp)]
    v_specs = [pl.BlockSpec((None, None, DA_HEADS * page, LANES), page_map(i)) for i in range(pp)]
    grid_spec = pltpu.PrefetchScalarGridSpec(
        num_scalar_prefetch=1, grid=(nbatch, n_pages // pp),
        in_specs=[pl.BlockSpec((None, rows, DA_WIDTH), per_b),
                  pl.BlockSpec((None, DA_WIDTH, page), per_b),
                  pl.BlockSpec((None, DA_HEADS * page, LANES), per_b)] + kt_specs + v_specs + [lam_spec] * 4,
        out_specs=pl.BlockSpec((None, SAMPLE_TPAD, DA_WIDTH), per_b),
        scratch_shapes=[pltpu.VMEM((rows, LANES), F32), pltpu.VMEM((rows, LANES), F32),
                        pltpu.VMEM((rows, LANES), F32)])
    return pl.pallas_call(
        functools.partial(_attn_sample_kernel, pp=pp, page=page, lam_init=lam_init),
        out_shape=jax.ShapeDtypeStruct((nbatch, SAMPLE_TPAD, DA_WIDTH), F32),
        grid_spec=grid_spec, compiler_params=_cparams(("arbitrary", "arbitrary")),
        name="attn_sample")(pt_flat, qx, ktn, vn, *([cache_kt] * pp), *([cache_v2] * pp), *lams)


def _row_tile(n, want):
    t = min(n, want)
    while n % t:
        t //= 2
    return t


def kernel(x_prompt, x_sample, cache_k, cache_v, state_h, page_table, p_prompt, p_sample,
           norm1_g, w_in, lam_q1, lam_k1, lam_q2, lam_k2, subln_g, hg_lb_logits, hg_norm_g,
           w_out, norm2_g, w_up, w_down, ple_norm_g, w_ple_gate, w_ple_proj, final_norm_g):
    nb, seq, d_model = x_prompt.shape
    ns, t_new, _ = x_sample.shape
    depth = w_in.shape[0]
    n_pool, page = cache_k.shape[1], cache_k.shape[2]
    past_len = page_table.shape[1] * page
    n_p, n_s = nb * seq, ns * t_new
    assert w_in.shape[2] == N_GROUPS * DA_WIDTH and t_new <= SAMPLE_TPAD <= page
    assert seq % HG_CHUNK == 0 and t_new <= HG_CHUNK_SAMPLE

    tm_p = _row_tile(seq, 512)
    tm_s = n_s
    tb_h = _row_tile(seq, 4 * HG_CHUNK)
    tables_p = _rope_tables(jnp.arange(seq, dtype=jnp.int32))
    tables_s = _rope_tables(jnp.tile(past_len + jnp.arange(t_new, dtype=jnp.int32), ns))

    cache_kt = jnp.transpose(cache_k, (0, 1, 3, 4, 2)).reshape(depth, n_pool, DA_WIDTH, page)
    cache_v2 = cache_v.reshape(depth, n_pool, page * DA_HEADS, LANES)

    lane_map = np.arange(DA_WIDTH) // DA_HDIM
    qmask = np.zeros((2, DA_HEADS, 1, DA_WIDTH), np.float32)
    for sgn in range(2):
        for h in range(DA_HEADS):
            qmask[sgn, h, 0] = lane_map == 2 * h + sgn
    qmask = jnp.asarray(qmask, BF16)

    xp = x_prompt.reshape(n_p, d_model)
    xs = x_sample.reshape(n_s, d_model)
    kt_all = v_all = None
    hp_l, ks_l, vs_l, hs_l = [], [], [], []
    row1 = lambda a: a.reshape(1, -1)
    for i in range(depth):
        lam_init = 0.8 - 0.6 * math.exp(-0.3 * i)
        lams = [row1(lam_q1[i]), row1(lam_k1[i]), row1(lam_q2[i]), row1(lam_k2[i])]
        w_bf = w_in[i].astype(BF16)
        wkt_bf = w_in[i][:, DA_WIDTH:2 * DA_WIDTH].T.astype(BF16)
        wo_bf, wu_bf, wd_bf = w_out[i].astype(BF16), w_up[i].astype(BF16), w_down[i].astype(BF16)
        wg_bf, wp_bf = w_ple_gate[i].astype(BF16), w_ple_proj[i].astype(BF16)
        g1, sg, ng = row1(norm1_g[i]), row1(subln_g[i]), row1(hg_norm_g[i])
        n2, pn = row1(norm2_g[i]), row1(ple_norm_g[i])
        fn = row1(final_norm_g) if i == depth - 1 else None

        (q_bf, kt_all, kt_bf, v_all, v_bf, hq, hk, lg, hi, hg) = _proj_in(
            xp, g1, w_bf, wkt_bf, tables_p, hg_lb_logits, i, tm_p, seq,
            kt_prev=kt_all, v_prev=v_all, depth=depth, k_transposed=True)
        ao = _attn_prompt(q_bf, kt_bf, v_bf, lams, lam_init, nb, seq)
        ho, hstate = _hgrn(hq, hk, hi, lg, None, nb, seq, HG_CHUNK, tb_h)
        xp = _merge(ao, ho, hg, xp, sg, ng, wo_bf, lam_init, tm_p)
        xp = _mlp(xp, p_prompt[i].reshape(n_p, -1), n2, wu_bf, wd_bf, pn, wg_bf, wp_bf, fn, tm_p)
        hp_l.append(hstate)

        (q_bf, k_s, k_bf, v_s, v_bf, hq, hk, lg, hi, hg) = _proj_in(
            xs, g1, w_bf, None, tables_s, hg_lb_logits, i, tm_s, n_s, k_transposed=False)
        qpad = jnp.pad(q_bf.reshape(ns, 1, 1, t_new, DA_WIDTH),
                       ((0, 0), (0, 0), (0, 0), (0, SAMPLE_TPAD - t_new), (0, 0)))
        qx = (qpad * qmask).reshape(ns, 2 * DA_HEADS * SAMPLE_TPAD, DA_WIDTH)
        ktn = jnp.pad(jnp.swapaxes(k_bf.reshape(ns, t_new, DA_WIDTH), 1, 2),
                      ((0, 0), (0, 0), (0, page - t_new)))
        vn = jnp.pad(v_s.reshape(ns, t_new, DA_WIDTH), ((0, 0), (0, page - t_new), (0, 0)))
        vn = vn.reshape(ns, page * DA_HEADS, LANES)
        ao8 = _attn_sample(page_table, qx, ktn, vn, cache_kt, cache_v2, lams, lam_init, i)
        ao = ao8[:, :t_new].reshape(n_s, DA_WIDTH)
        padc = lambda a: jnp.pad(a.reshape(ns, t_new, HG_WIDTH),
                                 ((0, 0), (0, HG_CHUNK_SAMPLE - t_new), (0, 0))).reshape(-1, HG_WIDTH)
        ho16, hstate = _hgrn(padc(hq), padc(hk), padc(hi), padc(lg), state_h[i], ns,
                             HG_CHUNK_SAMPLE, HG_CHUNK_SAMPLE, HG_CHUNK_SAMPLE)
        ho = ho16.reshape(ns, HG_CHUNK_SAMPLE, HG_WIDTH)[:, :t_new].reshape(n_s, HG_WIDTH)
        xs = _merge(ao, ho, hg, xs, sg, ng, wo_bf, lam_init, tm_s)
        xs = _mlp(xs, p_sample[i].reshape(n_s, -1), n2, wu_bf, wd_bf, pn, wg_bf, wp_bf, fn, tm_s)
        ks_l.append(k_s.reshape(ns, t_new, 2 * DA_HEADS, DA_HDIM))
        vs_l.append(v_s.reshape(ns, t_new, DA_HEADS, 2 * DA_HDIM))
        hs_l.append(hstate)

    k_prompt = jnp.transpose(kt_all.reshape(depth, nb, 2 * DA_HEADS, DA_HDIM, seq), (0, 1, 4, 2, 3))
    v_prompt = v_all.reshape(depth, nb, seq, DA_HEADS, 2 * DA_HDIM)
    return (xp.reshape(nb, seq, d_model), xs.reshape(ns, t_new, d_model), k_prompt, v_prompt,
            jnp.stack(hp_l), jnp.stack(ks_l), jnp.stack(vs_l), jnp.stack(hs_l))
```

```python
import functools
import math

import numpy as np
import jax
import jax.numpy as jnp
from jax import lax
from jax.experimental import pallas as pl
from jax.experimental.pallas import tpu as pltpu

F32 = jnp.float32
BF16 = jnp.bfloat16

DA_HEADS = 4
DA_HDIM = 64
DA_WIDTH = 2 * DA_HEADS * DA_HDIM
HG_HEADS = 4
HG_DIM = 128
HG_WIDTH = HG_HEADS * HG_DIM
N_GROUPS = 7
ROPE_THETA = 500000.0
ROT_DIM = DA_HDIM // 4
ROT_HALF = ROT_DIM // 2
NORM_EPS = 1e-6
SUBLN_EPS = 1e-5
MASK_VALUE = -1e30
LANES = 128
VMEM_LIMIT = 56 * 1024 * 1024

HG_CHUNK = 64
HG_UNROLL = 2
HG_CHUNK_SAMPLE = 16
ATTN_TQ = 256
ATTN_TK = 512
ATTN_UNROLL = 4
PAGES_PER_STEP = 16
SAMPLE_TPAD = 8


def _cparams(sem):
    return pltpu.CompilerParams(dimension_semantics=sem, vmem_limit_bytes=VMEM_LIMIT)


def _dot(a, b):
    return jnp.dot(a, b, preferred_element_type=F32)


def _dot_nt(a, b):
    return lax.dot_general(a, b, (((1,), (1,)), ((), ())), preferred_element_type=F32)


def _dot_tn(a, b):
    return lax.dot_general(a, b, (((0,), (0,)), ((), ())), preferred_element_type=F32)


def _rms(x, eps):
    return x * lax.rsqrt(jnp.mean(x * x, axis=-1, keepdims=True) + eps)


def _sigmoid(x):
    return 1.0 / (1.0 + jnp.exp(-x))


def _rope_rows(z, c_ref, s1_ref, s2_ref):
    c, s1, s2 = c_ref[...], s1_ref[...], s2_ref[...]
    outs = []
    for j in range(DA_WIDTH // LANES):
        zc = z[:, j * LANES:(j + 1) * LANES]
        up = pltpu.roll(zc, LANES - ROT_HALF, 1)
        dn = pltpu.roll(zc, ROT_HALF, 1)
        outs.append(zc * c + up * s1 + dn * s2)
    return jnp.concatenate(outs, axis=1)


def _proj_in_kernel(*refs, layer, k_transposed):
    if k_transposed:
        (x_ref, g_ref, w_ref, wkt_ref, c_ref, s1_ref, s2_ref, ct_ref, st_ref, lbl_ref,
         _kt_alias, _v_alias,
         q_ref, kt_ref, ktb_ref, v_ref, vb_ref, hq_ref, hk_ref, lg_ref, hi_ref, hg_ref) = refs
    else:
        (x_ref, g_ref, w_ref, c_ref, s1_ref, s2_ref, lbl_ref,
         q_ref, k_ref, kb_ref, v_ref, vb_ref, hq_ref, hk_ref, lg_ref, hi_ref, hg_ref) = refs
    xb = (_rms(x_ref[...], NORM_EPS) * g_ref[...]).astype(BF16)

    def cols(g):
        return _dot(xb, w_ref[:, g * DA_WIDTH:(g + 1) * DA_WIDTH])

    q_ref[...] = (_rope_rows(cols(0), c_ref, s1_ref, s2_ref) * (DA_HDIM ** -0.5)).astype(BF16)
    if k_transposed:
        zk = _dot_nt(wkt_ref[...], xb)
        kt_ref[...] = zk
        ktb_ref[...] = zk.astype(BF16)
        ct, st = ct_ref[...], st_ref[...]
        for j in range(2 * DA_HEADS):
            r = j * DA_HDIM
            x1 = zk[r:r + ROT_HALF]
            x2 = zk[r + ROT_HALF:r + ROT_DIM]
            o1 = x1 * ct - x2 * st
            o2 = x2 * ct + x1 * st
            kt_ref[r:r + ROT_HALF, :] = o1
            kt_ref[r + ROT_HALF:r + ROT_DIM, :] = o2
            ktb_ref[r:r + ROT_DIM, :] = jnp.concatenate([o1, o2], axis=0).astype(BF16)
    else:
        k = _rope_rows(cols(1), c_ref, s1_ref, s2_ref)
        k_ref[...] = k
        kb_ref[...] = k.astype(BF16)
    v = cols(2)
    if k_transposed:
        for h in range(DA_HEADS):
            v_ref[:, h, :] = v[:, h * LANES:(h + 1) * LANES]
    else:
        v_ref[...] = v
    vb_ref[...] = v.astype(BF16)
    zq = cols(3)
    hq_ref[...] = zq * _sigmoid(zq)
    lbl = lbl_ref[...]
    e = jnp.exp(lbl - jnp.max(lbl, axis=0, keepdims=True))
    sm = e / jnp.sum(e, axis=0, keepdims=True)
    lb = jnp.zeros_like(sm[0:1])
    for l in range(1, layer + 1):
        lb = lb + sm[l:l + 1]
    hk = (1.0 - lb) * _sigmoid(-cols(4))
    hk_ref[...] = hk
    lg_ref[...] = jnp.log1p(-hk)
    hi_ref[...] = cols(5)
    hg_ref[...] = cols(6)


def _rope_tables(pos):
    inv = ROPE_THETA ** (-(jnp.arange(ROT_HALF, dtype=F32) * 2.0) / ROT_DIM)
    ang = pos.astype(F32)[:, None] * inv[None, :]
    cos, sin = jnp.cos(ang), jnp.sin(ang)
    d = np.arange(LANES) % DA_HDIM
    f = d % ROT_HALF
    cosl, sinl = cos[:, f], sin[:, f]
    c = jnp.where((d < ROT_DIM)[None, :], cosl, 1.0)
    s1 = jnp.where((d < ROT_HALF)[None, :], -sinl, 0.0)
    s2 = jnp.where(((d >= ROT_HALF) & (d < ROT_DIM))[None, :], sinl, 0.0)
    return c, s1, s2, cos.T, sin.T


def _proj_in(x2d, g_row, w_bf, wkt_bf, tables, lbl, layer, tm, period, kt_prev=None, v_prev=None,
             depth=None, k_transposed=True):
    n, d = x2d.shape
    npt = period // tm
    nb = n // period
    c, s1, s2, ct, st = tables
    row = lambda i: (i, 0)
    const = lambda i: (0, 0)
    tab = lambda i: (i % npt, 0)
    in_specs = [pl.BlockSpec((tm, d), row), pl.BlockSpec((1, d), const),
                pl.BlockSpec(w_bf.shape, const)]
    args = [x2d, g_row, w_bf]
    if k_transposed:
        in_specs.append(pl.BlockSpec(wkt_bf.shape, const))
        args.append(wkt_bf)
    in_specs += [pl.BlockSpec((tm, LANES), tab)] * 3
    args += [c, s1, s2]
    if k_transposed:
        in_specs += [pl.BlockSpec((ROT_HALF, tm), lambda i: (0, i % npt))] * 2
        args += [ct, st]
    in_specs.append(pl.BlockSpec(lbl.shape, const))
    args.append(lbl)
    act = jax.ShapeDtypeStruct((n, DA_WIDTH), F32)
    act_b = jax.ShapeDtypeStruct((n, DA_WIDTH), BF16)
    act_spec = pl.BlockSpec((tm, DA_WIDTH), row)
    aliases = {}
    if k_transposed:
        kt_shape = jax.ShapeDtypeStruct((depth, nb, DA_WIDTH, period), F32)
        v_shape = jax.ShapeDtypeStruct((depth, n, DA_HEADS, LANES), F32)
        if kt_prev is not None:
            aliases = {len(args): 1, len(args) + 1: 3}
            in_specs += [pl.BlockSpec(memory_space=pl.ANY)] * 2
            args += [kt_prev, v_prev]
        else:
            in_specs += [pl.BlockSpec((1, d), const)] * 2
            args += [g_row, g_row]
        out_shape = [act_b, kt_shape, jax.ShapeDtypeStruct((nb, DA_WIDTH, period), BF16),
                     v_shape, act_b] + [act] * 5
        out_specs = [act_spec,
                     pl.BlockSpec((None, None, DA_WIDTH, tm), lambda i: (layer, i // npt, 0, i % npt)),
                     pl.BlockSpec((None, DA_WIDTH, tm), lambda i: (i // npt, 0, i % npt)),
                     pl.BlockSpec((None, tm, DA_HEADS, LANES), lambda i: (layer, i, 0, 0)),
                     act_spec] + [act_spec] * 5
    else:
        out_shape = [act_b, act, act_b, act, act_b] + [act] * 5
        out_specs = [act_spec] * 10
    return pl.pallas_call(
        functools.partial(_proj_in_kernel, layer=layer, k_transposed=k_transposed),
        out_shape=out_shape, grid=(n // tm,), in_specs=in_specs, out_specs=out_specs,
        input_output_aliases=aliases, compiler_params=_cparams(("arbitrary",)),
        name="proj_in_prompt" if k_transposed else "proj_in_sample")(*args)


def _lam_value(lq1, lk1, lq2, lk2, lam_init):
    a = jnp.sum(lq1[...] * lk1[...], axis=-1, keepdims=True)
    b = jnp.sum(lq2[...] * lk2[...], axis=-1, keepdims=True)
    return jnp.exp(a) - jnp.exp(b) + lam_init


def _attn_prompt_kernel(q_ref, kt_ref, v_ref, lq1, lk1, lq2, lk2, o_ref, qq_ref, m_ref, acc_ref,
                        *, tq, tk, unroll, lam_init):
    qi = pl.program_id(2)
    q = q_ref[...]
    lane = lax.broadcasted_iota(jnp.int32, q.shape, 1)
    zero = jnp.zeros_like(q)
    qq_ref[0:tq, :] = jnp.where(lane < DA_HDIM, q, zero)
    qq_ref[tq:2 * tq, :] = jnp.where(lane >= DA_HDIM, q, zero)
    m_ref[...] = jnp.full(m_ref.shape, MASK_VALUE, F32)
    acc_ref[...] = jnp.zeros(acc_ref.shape, F32)
    ones = jnp.ones((tk, LANES), BF16)

    def block(kb, masked):
        off = pl.multiple_of(kb * tk, tk)
        s = _dot(qq_ref[...], kt_ref[:, pl.ds(off, tk)])
        if masked:
            row = qi * tq + lax.broadcasted_iota(jnp.int32, s.shape, 0) % tq
            col = off + lax.broadcasted_iota(jnp.int32, s.shape, 1)
            s = jnp.where(col <= row, s, MASK_VALUE)
        m_prev = m_ref[...]
        m_next = jnp.maximum(m_prev, jnp.max(s, axis=1, keepdims=True))
        alpha = jnp.exp(m_prev - m_next)
        p = jnp.exp(s - jnp.concatenate([m_next] * (tk // LANES), axis=1)).astype(BF16)
        vx = jnp.concatenate([v_ref[pl.ds(off, tk), :], ones], axis=1)
        acc_ref[...] = acc_ref[...] * jnp.concatenate([alpha, alpha], axis=1) + _dot(p, vx)
        m_ref[...] = m_next

    nfull = (qi * tq) // tk
    ngroup = nfull // unroll

    def group(gi, carry):
        for u in range(unroll):
            block(gi * unroll + u, False)
        return carry

    def single(kb, carry):
        block(kb, False)
        return carry

    lax.fori_loop(0, ngroup, group, 0)
    lax.fori_loop(ngroup * unroll, nfull, single, 0)
    block(nfull, True)
    acc = acc_ref[...]
    o1 = acc[0:tq, 0:LANES] / acc[0:tq, LANES:]
    o2 = acc[tq:, 0:LANES] / acc[tq:, LANES:]
    o_ref[...] = o1 - _lam_value(lq1, lk1, lq2, lk2, lam_init) * o2


def _attn_prompt(q_bf, kt_bf, v_bf, lams, lam_init, nb, seq):
    tq, tk = min(ATTN_TQ, seq), min(ATTN_TK, seq)
    assert tk % tq == 0
    nq = seq // tq
    lam_spec = pl.BlockSpec((1, DA_HDIM), lambda b, h, i: (0, 0))
    return pl.pallas_call(
        functools.partial(_attn_prompt_kernel, tq=tq, tk=tk, unroll=ATTN_UNROLL, lam_init=lam_init),
        out_shape=jax.ShapeDtypeStruct(q_bf.shape, F32),
        grid=(nb, DA_HEADS, nq),
        in_specs=[pl.BlockSpec((tq, LANES), lambda b, h, i: (b * nq + i, h)),
                  pl.BlockSpec((None, LANES, seq), lambda b, h, i: (b, h, 0)),
                  pl.BlockSpec((seq, LANES), lambda b, h, i: (b, h))] + [lam_spec] * 4,
        out_specs=pl.BlockSpec((tq, LANES), lambda b, h, i: (b * nq + i, h)),
        scratch_shapes=[pltpu.VMEM((2 * tq, LANES), BF16), pltpu.VMEM((2 * tq, LANES), F32),
                        pltpu.VMEM((2 * tq, 2 * LANES), F32)],
        compiler_params=_cparams(("arbitrary",) * 3), name="attn_prompt")(q_bf, kt_bf, v_bf, *lams)


def _hgrn_consts(c):
    nl = int(math.log2(c))
    t = np.arange(c)
    lv = np.zeros((nl, c, c), np.float32)
    msk = np.zeros((nl + 1, c, c), np.float32)
    for l in range(nl):
        b = 1 << l
        mid = (t // (2 * b)) * 2 * b + b
        upper = t >= mid
        for i in range(c):
            if upper[i]:
                lv[l, i, mid[i]:i + 1] = 1.0
            else:
                lv[l, i, i + 1:mid[i]] = 1.0
        same = (t[:, None] // (2 * b)) == (t[None, :] // (2 * b))
        msk[l] = same & upper[:, None] & (~upper)[None, :]
    msk[nl] = np.eye(c)
    tri = np.tril(np.ones((c, c), np.float32))
    rest = np.triu(np.ones((c, c), np.float32), 1)
    d = np.concatenate([lv.reshape(nl * c, c), tri, rest], axis=0)
    return jnp.asarray(d, BF16), jnp.asarray(msk, F32), nl


def _hgrn_kernel(*refs, c, nl, nc, has_s0):
    if has_s0:
        hq_ref, hk_ref, hv_ref, lg_ref, d_ref, msk_ref, s0_ref, o_ref, sout_ref, st_ref = refs
    else:
        hq_ref, hk_ref, hv_ref, lg_ref, d_ref, msk_ref, o_ref, sout_ref, st_ref = refs
    jb = pl.program_id(1)
    heads = range(HG_HEADS)

    @pl.when(jb == 0)
    def _():
        for h in heads:
            st_ref[h] = s0_ref[h].T if has_s0 else jnp.zeros((HG_DIM, HG_DIM), F32)

    def chunk(ci, carry):
        r0 = pl.multiple_of(ci * c, c)
        sls = [(pl.ds(r0, c), slice(h * HG_DIM, (h + 1) * HG_DIM)) for h in heads]
        q = [hq_ref[sl] for sl in sls]
        k = [hk_ref[sl] for sl in sls]
        vb = [hv_ref[sl].astype(BF16) for sl in sls]
        st = [st_ref[h] for h in heads]
        e = []
        for sl in sls:
            g = lg_ref[sl]
            g_hi = g.astype(BF16)
            g_lo = (g - g_hi.astype(F32)).astype(BF16)
            e2 = _dot(d_ref[...], jnp.concatenate([g_hi, g_lo], axis=1))
            e.append(e2[:, 0:HG_DIM] + e2[:, HG_DIM:])
        a = [_dot_nt(q[h].astype(BF16), k[h].astype(BF16)) * msk_ref[nl] for h in heads]
        for l in range(nl):
            for h in heads:
                w = jnp.exp(e[h][l * c:(l + 1) * c])
                a[h] = a[h] + _dot_nt((q[h] * w).astype(BF16), (k[h] * w).astype(BF16)) * msk_ref[l]
        for h in heads:
            gcum = e[h][nl * c:(nl + 1) * c]
            grest = e[h][(nl + 1) * c:(nl + 2) * c]
            o_ref[sls[h]] = (_dot(a[h].astype(BF16), vb[h])
                             + _dot_nt((q[h] * jnp.exp(gcum)).astype(BF16), st[h].astype(BF16)))
            khat = (k[h] * jnp.exp(grest)).astype(BF16)
            st_ref[h] = st[h] * jnp.exp(gcum[c - 1:c, :]) + _dot_tn(vb[h], khat)
        return carry

    lax.fori_loop(0, nc, chunk, 0, unroll=min(nc, HG_UNROLL))

    @pl.when(jb == pl.num_programs(1) - 1)
    def _():
        for h in heads:
            sout_ref[h] = st_ref[h].T


def _hgrn(hq, hk, hv, lg, s0, nb, seq, c, tb):
    d, msk, nl = _hgrn_consts(c)
    nblk = seq // tb
    row = pl.BlockSpec((tb, HG_WIDTH), lambda b, j: (b * nblk + j, 0))
    st_spec = pl.BlockSpec((None, HG_HEADS, HG_DIM, HG_DIM), lambda b, j: (b, 0, 0, 0))
    in_specs = [row] * 4 + [pl.BlockSpec(d.shape, lambda b, j: (0, 0)),
                            pl.BlockSpec(msk.shape, lambda b, j: (0, 0, 0))]
    args = [hq, hk, hv, lg, d, msk]
    if s0 is not None:
        in_specs.append(st_spec)
        args.append(s0)
    return pl.pallas_call(
        functools.partial(_hgrn_kernel, c=c, nl=nl, nc=tb // c, has_s0=s0 is not None),
        out_shape=[jax.ShapeDtypeStruct(hq.shape, F32),
                   jax.ShapeDtypeStruct((nb, HG_HEADS, HG_DIM, HG_DIM), F32)],
        grid=(nb, nblk), in_specs=in_specs, out_specs=[row, st_spec],
        scratch_shapes=[pltpu.VMEM((HG_HEADS, HG_DIM, HG_DIM), F32)],
        compiler_params=_cparams(("arbitrary", "arbitrary")),
        name="hgrn_sample" if s0 is not None else "hgrn_prompt")(*args)


def _channel_kernel(*refs, lam_init, ff_chunk, last):
    (ao_ref, ho_ref, hg_ref, x_ref, pe_ref, sg_ref, ng_ref, wo_ref, n2_ref, wu_ref, wd_ref,
     pn_ref, wg_ref, wp_ref) = refs[:14]
    parts = []
    for h in range(DA_HEADS):
        a = ao_ref[:, h * LANES:(h + 1) * LANES]
        parts.append(_rms(a, SUBLN_EPS) * sg_ref[...] * (1.0 - lam_init))
    for h in range(HG_HEADS):
        sl = slice(h * HG_DIM, (h + 1) * HG_DIM)
        gate = hg_ref[:, sl]
        parts.append(_rms(ho_ref[:, sl], NORM_EPS) * ng_ref[...] * (gate * _sigmoid(gate)))
    x = x_ref[...] + _dot(jnp.concatenate(parts, axis=1).astype(BF16), wo_ref[...])
    hn = (_rms(x, NORM_EPS) * n2_ref[...]).astype(BF16)
    acc = x
    for c0 in range(0, wu_ref.shape[1], ff_chunk):
        u = jnp.maximum(_dot(hn, wu_ref[:, c0:c0 + ff_chunk]), 0.0)
        acc = acc + _dot((u * u).astype(BF16), wd_ref[c0:c0 + ff_chunk, :])
    gate = _sigmoid(_dot((_rms(acc, NORM_EPS) * pn_ref[...]).astype(BF16), wg_ref[...]))
    y = acc + gate * _dot(pe_ref[...].astype(BF16), wp_ref[...])
    if last:
        fn_ref, o_ref = refs[14:]
        o_ref[...] = _rms(y, NORM_EPS) * fn_ref[...]
    else:
        refs[14][...] = y


def _resident(shape):
    return pl.BlockSpec(shape, lambda i: (0,) * len(shape), pipeline_mode=pl.Buffered(1))


def _channel(ao, ho, hg, x2d, pe, sg_row, ng_row, wo_bf, n2_row, wu_bf, wd_bf, pn_row, wg_bf, wp_bf,
             fn_row, lam_init, tm):
    n, d = x2d.shape
    row = lambda i: (i, 0)
    half = pl.BlockSpec((tm, DA_WIDTH), row)
    args = [ao, ho, hg, x2d, pe, sg_row, ng_row, wo_bf, n2_row, wu_bf, wd_bf, pn_row, wg_bf, wp_bf]
    in_specs = [half, half, half, pl.BlockSpec((tm, d), row), pl.BlockSpec((tm, pe.shape[1]), row)]
    in_specs += [_resident(a.shape) for a in args[5:]]
    if fn_row is not None:
        in_specs.append(_resident(fn_row.shape))
        args.append(fn_row)
    return pl.pallas_call(
        functools.partial(_channel_kernel, lam_init=lam_init, ff_chunk=d, last=fn_row is not None),
        out_shape=jax.ShapeDtypeStruct((n, d), F32), grid=(n // tm,),
        in_specs=in_specs, out_specs=pl.BlockSpec((tm, d), row),
        compiler_params=_cparams(("arbitrary",)), name="channel_mix")(*args)


def _attn_sample_kernel(pt_ref, qx_ref, ktn_ref, vn_ref, *refs, pp, page, lam_init):
    kt_refs = refs[:pp]
    v_refs = refs[pp:2 * pp]
    lq1, lk1, lq2, lk2, o_ref, m_ref, l_ref, acc_ref = refs[2 * pp:]
    j = pl.program_id(1)
    rows = qx_ref.shape[0]
    qx = qx_ref[...]
    rhead = (lax.broadcasted_iota(jnp.int32, (rows, LANES), 0) // SAMPLE_TPAD) % DA_HEADS

    def update(kts, vs, mask):
        s = jnp.concatenate([_dot(qx, kt[...].astype(BF16)) for kt in kts], axis=1)
        if mask is not None:
            s = jnp.where(mask, s, MASK_VALUE)
        m_prev = m_ref[...]
        m_next = jnp.maximum(m_prev, jnp.max(s, axis=1, keepdims=True))
        alpha = jnp.exp(m_prev - m_next)
        p = jnp.exp(s - jnp.concatenate([m_next] * len(kts), axis=1))
        l_ref[...] = l_ref[...] * alpha + jnp.sum(p, axis=1, keepdims=True)
        pb = p.astype(BF16)
        acc = acc_ref[...] * alpha
        for i, v_ref in enumerate(vs):
            pi = pb[:, i * page:(i + 1) * page]
            for h in range(DA_HEADS):
                vh = v_ref[pl.ds(h, page, stride=DA_HEADS), :].astype(BF16)
                acc = acc + jnp.where(rhead == h, _dot(pi, vh), 0.0)
        acc_ref[...] = acc
        m_ref[...] = m_next

    @pl.when(j == 0)
    def _():
        m_ref[...] = jnp.full(m_ref.shape, MASK_VALUE, F32)
        l_ref[...] = jnp.zeros(l_ref.shape, F32)
        acc_ref[...] = jnp.zeros(acc_ref.shape, F32)
        tok = lax.broadcasted_iota(jnp.int32, (rows, page), 0) % SAMPLE_TPAD
        col = lax.broadcasted_iota(jnp.int32, (rows, page), 1)
        update([ktn_ref], [vn_ref], col <= tok)

    update(kt_refs, v_refs, None)

    @pl.when(j == pl.num_programs(1) - 1)
    def _():
        o = acc_ref[...] / l_ref[...]
        lam = _lam_value(lq1, lk1, lq2, lk2, lam_init)
        half = rows // 2
        for h in range(DA_HEADS):
            pos = o[h * SAMPLE_TPAD:(h + 1) * SAMPLE_TPAD]
            neg = o[half + h * SAMPLE_TPAD:half + (h + 1) * SAMPLE_TPAD]
            o_ref[:, h * LANES:(h + 1) * LANES] = pos - lam * neg


def _attn_sample(page_table, qx, ktn, vn, cache_kt, cache_v2, lams, lam_init, layer):
    nbatch, n_pages = page_table.shape
    page = cache_kt.shape[-1]
    pp = PAGES_PER_STEP
    while n_pages % pp:
        pp //= 2
    rows = qx.shape[1]
    pt_flat = page_table.reshape(-1)

    def page_map(i):
        return lambda b, j, pt: (layer, pt[b * n_pages + j * pp + i], 0, 0)

    per_b = lambda b, j, pt: (b, 0, 0)
    lam_spec = pl.BlockSpec((1, DA_HDIM), lambda b, j, pt: (0, 0))
    kt_specs = [pl.BlockSpec((None, None, DA_WIDTH, page), page_map(i)) for i in range(pp)]
    v_specs = [pl.BlockSpec((None, None, DA_HEADS * page, LANES), page_map(i)) for i in range(pp)]
    grid_spec = pltpu.PrefetchScalarGridSpec(
        num_scalar_prefetch=1, grid=(nbatch, n_pages // pp),
        in_specs=[pl.BlockSpec((None, rows, DA_WIDTH), per_b),
                  pl.BlockSpec((None, DA_WIDTH, page), per_b),
                  pl.BlockSpec((None, DA_HEADS * page, LANES), per_b)] + kt_specs + v_specs + [lam_spec] * 4,
        out_specs=pl.BlockSpec((None, SAMPLE_TPAD, DA_WIDTH), per_b),
        scratch_shapes=[pltpu.VMEM((rows, LANES), F32), pltpu.VMEM((rows, LANES), F32),
                        pltpu.VMEM((rows, LANES), F32)])
    return pl.pallas_call(
        functools.partial(_attn_sample_kernel, pp=pp, page=page, lam_init=lam_init),
        out_shape=jax.ShapeDtypeStruct((nbatch, SAMPLE_TPAD, DA_WIDTH), F32),
        grid_spec=grid_spec, compiler_params=_cparams(("arbitrary", "arbitrary")),
        name="attn_sample")(pt_flat, qx, ktn, vn, *([cache_kt] * pp), *([cache_v2] * pp), *lams)


def _row_tile(n, want):
    t = min(n, want)
    while n % t:
        t //= 2
    return t


def kernel(x_prompt, x_sample, cache_k, cache_v, state_h, page_table, p_prompt, p_sample,
           norm1_g, w_in, lam_q1, lam_k1, lam_q2, lam_k2, subln_g, hg_lb_logits, hg_norm_g,
           w_out, norm2_g, w_up, w_down, ple_norm_g, w_ple_gate, w_ple_proj, final_norm_g):
    nb, seq, d_model = x_prompt.shape
    ns, t_new, _ = x_sample.shape
    depth = w_in.shape[0]
    n_pool, page = cache_k.shape[1], cache_k.shape[2]
    past_len = page_table.shape[1] * page
    n_p, n_s = nb * seq, ns * t_new
    assert w_in.shape[2] == N_GROUPS * DA_WIDTH and t_new <= SAMPLE_TPAD <= page
    assert seq % HG_CHUNK == 0 and t_new <= HG_CHUNK_SAMPLE

    tm_p = _row_tile(seq, 512)
    tm_s = n_s
    tb_h = _row_tile(seq, 4 * HG_CHUNK)
    tables_p = _rope_tables(jnp.arange(seq, dtype=jnp.int32))
    tables_s = _rope_tables(jnp.tile(past_len + jnp.arange(t_new, dtype=jnp.int32), ns))

    cache_kt = jnp.transpose(cache_k, (0, 1, 3, 4, 2)).reshape(depth, n_pool, DA_WIDTH, page)
    cache_v2 = cache_v.reshape(depth, n_pool, page * DA_HEADS, LANES)

    lane_map = np.arange(DA_WIDTH) // DA_HDIM
    qmask = np.zeros((2, DA_HEADS, 1, DA_WIDTH), np.float32)
    for sgn in range(2):
        for h in range(DA_HEADS):
            qmask[sgn, h, 0] = lane_map == 2 * h + sgn
    qmask = jnp.asarray(qmask, BF16)

    xp = x_prompt.reshape(n_p, d_model)
    xs = x_sample.reshape(n_s, d_model)
    kt_all = v_all = None
    hp_l, ks_l, vs_l, hs_l = [], [], [], []
    row1 = lambda a: a.reshape(1, -1)
    for i in range(depth):
        lam_init = 0.8 - 0.6 * math.exp(-0.3 * i)
        lams = [row1(lam_q1[i]), row1(lam_k1[i]), row1(lam_q2[i]), row1(lam_k2[i])]
        w_bf = w_in[i].astype(BF16)
        wkt_bf = w_in[i][:, DA_WIDTH:2 * DA_WIDTH].T.astype(BF16)
        wo_bf, wu_bf, wd_bf = w_out[i].astype(BF16), w_up[i].astype(BF16), w_down[i].astype(BF16)
        wg_bf, wp_bf = w_ple_gate[i].astype(BF16), w_ple_proj[i].astype(BF16)
        g1, sg, ng = row1(norm1_g[i]), row1(subln_g[i]), row1(hg_norm_g[i])
        n2, pn = row1(norm2_g[i]), row1(ple_norm_g[i])
        fn = row1(final_norm_g) if i == depth - 1 else None

        (q_bf, kt_all, kt_bf, v_all, v_bf, hq, hk, lg, hi, hg) = _proj_in(
            xp, g1, w_bf, wkt_bf, tables_p, hg_lb_logits, i, tm_p, seq,
            kt_prev=kt_all, v_prev=v_all, depth=depth, k_transposed=True)
        ao = _attn_prompt(q_bf, kt_bf, v_bf, lams, lam_init, nb, seq)
        ho, hstate = _hgrn(hq, hk, hi, lg, None, nb, seq, HG_CHUNK, tb_h)
        xp = _channel(ao, ho, hg, xp, p_prompt[i].reshape(n_p, -1), sg, ng, wo_bf, n2, wu_bf, wd_bf,
                      pn, wg_bf, wp_bf, fn, lam_init, tm_p)
        hp_l.append(hstate)

        (q_bf, k_s, k_bf, v_s, v_bf, hq, hk, lg, hi, hg) = _proj_in(
            xs, g1, w_bf, None, tables_s, hg_lb_logits, i, tm_s, n_s, k_transposed=False)
        qpad = jnp.pad(q_bf.reshape(ns, 1, 1, t_new, DA_WIDTH),
                       ((0, 0), (0, 0), (0, 0), (0, SAMPLE_TPAD - t_new), (0, 0)))
        qx = (qpad * qmask).reshape(ns, 2 * DA_HEADS * SAMPLE_TPAD, DA_WIDTH)
        ktn = jnp.pad(jnp.swapaxes(k_bf.reshape(ns, t_new, DA_WIDTH), 1, 2),
                      ((0, 0), (0, 0), (0, page - t_new)))
        vn = jnp.pad(v_s.reshape(ns, t_new, DA_WIDTH), ((0, 0), (0, page - t_new), (0, 0)))
        vn = vn.reshape(ns, page * DA_HEADS, LANES)
        ao8 = _attn_sample(page_table, qx, ktn, vn, cache_kt, cache_v2, lams, lam_init, i)
        ao = ao8[:, :t_new].reshape(n_s, DA_WIDTH)
        padc = lambda a: jnp.pad(a.reshape(ns, t_new, HG_WIDTH),
                                 ((0, 0), (0, HG_CHUNK_SAMPLE - t_new), (0, 0))).reshape(-1, HG_WIDTH)
        ho16, hstate = _hgrn(padc(hq), padc(hk), padc(hi), padc(lg), state_h[i], ns,
                             HG_CHUNK_SAMPLE, HG_CHUNK_SAMPLE, HG_CHUNK_SAMPLE)
        ho = ho16.reshape(ns, HG_CHUNK_SAMPLE, HG_WIDTH)[:, :t_new].reshape(n_s, HG_WIDTH)
        xs = _channel(ao, ho, hg, xs, p_sample[i].reshape(n_s, -1), sg, ng, wo_bf, n2, wu_bf, wd_bf,
                      pn, wg_bf, wp_bf, fn, lam_init, tm_s)
        ks_l.append(k_s.reshape(ns, t_new, 2 * DA_HEADS, DA_HDIM))
        vs_l.append(v_s.reshape(ns, t_new, DA_HEADS, 2 * DA_HDIM))
        hs_l.append(hstate)

    k_prompt = jnp.transpose(kt_all.reshape(depth, nb, 2 * DA_HEADS, DA_HDIM, seq), (0, 1, 4, 2, 3))
    v_prompt = v_all.reshape(depth, nb, seq, DA_HEADS, 2 * DA_HDIM)
    return (xp.reshape(nb, seq, d_model), xs.reshape(ns, t_new, d_model), k_prompt, v_prompt,
            jnp.stack(hp_l), jnp.stack(ks_l), jnp.stack(vs_l), jnp.stack(hs_l))
```

```python
import functools
import math

import numpy as np
import jax
import jax.numpy as jnp
from jax import lax
from jax.experimental import pallas as pl
from jax.experimental.pallas import tpu as pltpu

F32 = jnp.float32
BF16 = jnp.bfloat16

DA_HEADS = 4
DA_HDIM = 64
DA_WIDTH = 2 * DA_HEADS * DA_HDIM
HG_HEADS = 4
HG_DIM = 128
HG_WIDTH = HG_HEADS * HG_DIM
N_GROUPS = 7
ROPE_THETA = 500000.0
ROT_DIM = DA_HDIM // 4
ROT_HALF = ROT_DIM // 2
NORM_EPS = 1e-6
SUBLN_EPS = 1e-5
MASK_VALUE = -1e30
QK_SCALE_LOG2 = DA_HDIM ** -0.5 * math.log2(math.e)
LANES = 128
VMEM_LIMIT = 56 * 1024 * 1024

HG_CHUNK = 64
HG_UNROLL = 4
HG_CHUNK_SAMPLE = 16
ATTN_TQ = 1024
ATTN_TK = 256
ATTN_GROUPS = (8, 4, 2, 1)
PAGES_PER_STEP = 32
SAMPLE_TPAD = 8


def _cparams(sem):
    return pltpu.CompilerParams(dimension_semantics=sem, vmem_limit_bytes=VMEM_LIMIT)


def _dot(a, b):
    return jnp.dot(a, b, preferred_element_type=F32)


def _dot_nt(a, b):
    return lax.dot_general(a, b, (((1,), (1,)), ((), ())), preferred_element_type=F32)


def _dot_tn(a, b):
    return lax.dot_general(a, b, (((0,), (0,)), ((), ())), preferred_element_type=F32)


def _rms(x, eps):
    return x * lax.rsqrt(jnp.mean(x * x, axis=-1, keepdims=True) + eps)


def _sigmoid(x):
    return 1.0 / (1.0 + jnp.exp(-x))


def _rope_rows(z, c_ref, s1_ref, s2_ref):
    c, s1, s2 = c_ref[...], s1_ref[...], s2_ref[...]
    outs = []
    for j in range(DA_WIDTH // LANES):
        zc = z[:, j * LANES:(j + 1) * LANES]
        up = pltpu.roll(zc, LANES - ROT_HALF, 1)
        dn = pltpu.roll(zc, ROT_HALF, 1)
        outs.append(zc * c + up * s1 + dn * s2)
    return jnp.concatenate(outs, axis=1)


def _proj_in_kernel(*refs, layer, k_transposed):
    if k_transposed:
        (x_ref, g_ref, w_ref, wkt_ref, c_ref, s1_ref, s2_ref, ct_ref, st_ref, lbl_ref,
         _kt_alias, _v_alias,
         q_ref, kt_ref, ktb_ref, v_ref, vb_ref, hq_ref, hk_ref, lg_ref, hi_ref, hg_ref) = refs
    else:
        (x_ref, g_ref, w_ref, c_ref, s1_ref, s2_ref, lbl_ref,
         q_ref, k_ref, kb_ref, v_ref, vb_ref, hq_ref, hk_ref, lg_ref, hi_ref, hg_ref) = refs
    xb = (_rms(x_ref[...], NORM_EPS) * g_ref[...]).astype(BF16)

    def cols(g):
        return _dot(xb, w_ref[:, g * DA_WIDTH:(g + 1) * DA_WIDTH])

    q_ref[...] = (_rope_rows(cols(0), c_ref, s1_ref, s2_ref) * QK_SCALE_LOG2).astype(BF16)
    if k_transposed:
        zk = _dot_nt(wkt_ref[...], xb)
        kt_ref[...] = zk
        ktb_ref[...] = zk.astype(BF16)
        ct, st = ct_ref[...], st_ref[...]
        for j in range(2 * DA_HEADS):
            r = j * DA_HDIM
            x1 = zk[r:r + ROT_HALF]
            x2 = zk[r + ROT_HALF:r + ROT_DIM]
            o1 = x1 * ct - x2 * st
            o2 = x2 * ct + x1 * st
            kt_ref[r:r + ROT_HALF, :] = o1
            kt_ref[r + ROT_HALF:r + ROT_DIM, :] = o2
            ktb_ref[r:r + ROT_DIM, :] = jnp.concatenate([o1, o2], axis=0).astype(BF16)
    else:
        k = _rope_rows(cols(1), c_ref, s1_ref, s2_ref)
        k_ref[...] = k
        kb_ref[...] = k.astype(BF16)
    v = cols(2)
    if k_transposed:
        for h in range(DA_HEADS):
            v_ref[:, h, :] = v[:, h * LANES:(h + 1) * LANES]
    else:
        v_ref[...] = v
    vb_ref[...] = v.astype(BF16)
    zq = cols(3)
    hq_ref[...] = zq * _sigmoid(zq)
    lbl = lbl_ref[...]
    e = jnp.exp(lbl - jnp.max(lbl, axis=0, keepdims=True))
    sm = e / jnp.sum(e, axis=0, keepdims=True)
    lb = jnp.zeros_like(sm[0:1])
    for l in range(1, layer + 1):
        lb = lb + sm[l:l + 1]
    hk = (1.0 - lb) * _sigmoid(-cols(4))
    hk_ref[...] = hk
    lg_ref[...] = jnp.log1p(-hk)
    hi_ref[...] = cols(5)
    hg_ref[...] = cols(6)


def _rope_tables(pos):
    inv = ROPE_THETA ** (-(jnp.arange(ROT_HALF, dtype=F32) * 2.0) / ROT_DIM)
    ang = pos.astype(F32)[:, None] * inv[None, :]
    cos, sin = jnp.cos(ang), jnp.sin(ang)
    d = np.arange(LANES) % DA_HDIM
    f = d % ROT_HALF
    cosl, sinl = cos[:, f], sin[:, f]
    c = jnp.where((d < ROT_DIM)[None, :], cosl, 1.0)
    s1 = jnp.where((d < ROT_HALF)[None, :], -sinl, 0.0)
    s2 = jnp.where(((d >= ROT_HALF) & (d < ROT_DIM))[None, :], sinl, 0.0)
    return c, s1, s2, cos.T, sin.T


def _proj_in(x2d, g_row, w_bf, wkt_bf, tables, lbl, layer, tm, period, kt_prev=None, v_prev=None,
             depth=None, k_transposed=True):
    n, d = x2d.shape
    npt = period // tm
    nb = n // period
    c, s1, s2, ct, st = tables
    row = lambda i: (i, 0)
    const = lambda i: (0, 0)
    tab = lambda i: (i % npt, 0)
    in_specs = [pl.BlockSpec((tm, d), row), pl.BlockSpec((1, d), const),
                pl.BlockSpec(w_bf.shape, const)]
    args = [x2d, g_row, w_bf]
    if k_transposed:
        in_specs.append(pl.BlockSpec(wkt_bf.shape, const))
        args.append(wkt_bf)
    in_specs += [pl.BlockSpec((tm, LANES), tab)] * 3
    args += [c, s1, s2]
    if k_transposed:
        in_specs += [pl.BlockSpec((ROT_HALF, tm), lambda i: (0, i % npt))] * 2
        args += [ct, st]
    in_specs.append(pl.BlockSpec(lbl.shape, const))
    args.append(lbl)
    act = jax.ShapeDtypeStruct((n, DA_WIDTH), F32)
    act_b = jax.ShapeDtypeStruct((n, DA_WIDTH), BF16)
    act_spec = pl.BlockSpec((tm, DA_WIDTH), row)
    aliases = {}
    if k_transposed:
        kt_shape = jax.ShapeDtypeStruct((depth, nb, DA_WIDTH, period), F32)
        v_shape = jax.ShapeDtypeStruct((depth, n, DA_HEADS, LANES), F32)
        if kt_prev is not None:
            aliases = {len(args): 1, len(args) + 1: 3}
            in_specs += [pl.BlockSpec(memory_space=pl.ANY)] * 2
            args += [kt_prev, v_prev]
        else:
            in_specs += [pl.BlockSpec((1, d), const)] * 2
            args += [g_row, g_row]
        out_shape = [act_b, kt_shape, jax.ShapeDtypeStruct((nb, DA_WIDTH, period), BF16),
                     v_shape, act_b] + [act] * 5
        out_specs = [act_spec,
                     pl.BlockSpec((None, None, DA_WIDTH, tm), lambda i: (layer, i // npt, 0, i % npt)),
                     pl.BlockSpec((None, DA_WIDTH, tm), lambda i: (i // npt, 0, i % npt)),
                     pl.BlockSpec((None, tm, DA_HEADS, LANES), lambda i: (layer, i, 0, 0)),
                     act_spec] + [act_spec] * 5
    else:
        out_shape = [act_b, act, act_b, act, act_b] + [act] * 5
        out_specs = [act_spec] * 10
    return pl.pallas_call(
        functools.partial(_proj_in_kernel, layer=layer, k_transposed=k_transposed),
        out_shape=out_shape, grid=(n // tm,), in_specs=in_specs, out_specs=out_specs,
        input_output_aliases=aliases, compiler_params=_cparams(("arbitrary",)),
        name="proj_in_prompt" if k_transposed else "proj_in_sample")(*args)


def _lam_value(lq1, lk1, lq2, lk2, lam_init):
    a = jnp.sum(lq1[...] * lk1[...], axis=-1, keepdims=True)
    b = jnp.sum(lq2[...] * lk2[...], axis=-1, keepdims=True)
    return jnp.exp(a) - jnp.exp(b) + lam_init


def _attn_prompt_kernel(q_ref, kt_ref, v_ref, lq1, lk1, lq2, lk2, o_ref, qq_ref, m_ref, acc_ref,
                        *, tq, tk, lam_init):
    qi = pl.program_id(2)
    nr = tq // tk
    q = q_ref[...]
    lane = lax.broadcasted_iota(jnp.int32, (tk, LANES), 1)
    for r in range(nr):
        qr = q[r * tk:(r + 1) * tk]
        zero = jnp.zeros_like(qr)
        qq_ref[r, 0:tk, :] = jnp.where(lane < DA_HDIM, qr, zero)
        qq_ref[r, tk:2 * tk, :] = jnp.where(lane >= DA_HDIM, qr, zero)
    m_ref[...] = jnp.full(m_ref.shape, MASK_VALUE, F32)
    acc_ref[...] = jnp.zeros(acc_ref.shape, F32)
    ones = jnp.ones((tk, LANES), BF16)

    def block(kb, full, diag):
        off = pl.multiple_of(kb * tk, tk)
        kt = kt_ref[:, pl.ds(off, tk)]
        vx = jnp.concatenate([v_ref[pl.ds(off, tk), :], ones], axis=1)
        for r in list(full) + ([diag] if diag is not None else []):
            s = _dot(qq_ref[r], kt)
            if r == diag:
                row = lax.broadcasted_iota(jnp.int32, s.shape, 0) % tk
                col = lax.broadcasted_iota(jnp.int32, s.shape, 1)
                s = jnp.where(col <= row, s, MASK_VALUE)
            m_prev = m_ref[r]
            m_next = jnp.maximum(m_prev, jnp.max(s, axis=1, keepdims=True))
            alpha = jnp.exp2(m_prev - m_next)
            p = jnp.exp2(s - jnp.concatenate([m_next] * (tk // LANES), axis=1)).astype(BF16)
            acc_ref[r] = acc_ref[r] * jnp.concatenate([alpha, alpha], axis=1) + _dot(p, vx)
            m_ref[r] = m_next

    everyone = range(nr)
    nfull = qi * nr

    def run(base, count):
        for u in range(count):
            block(base + u, everyone, None)

    top = ATTN_GROUPS[0]
    ntop = nfull // top

    def top_group(gi, carry):
        run(gi * top, top)
        return carry

    lax.fori_loop(0, ntop, top_group, 0)
    done = ntop * top
    for size in ATTN_GROUPS[1:]:
        take = ((nfull - done) // size) % 2 if size > 1 else (nfull - done)
        base = done

        @pl.when(take > 0)
        def _(base=base, size=size):
            run(base, size)

        done = done + take * size
    for r in range(nr):
        block(nfull + r, range(r + 1, nr), r)
    lam = _lam_value(lq1, lk1, lq2, lk2, lam_init)
    for r in range(nr):
        acc = acc_ref[r]
        o1 = acc[0:tk, 0:LANES] / acc[0:tk, LANES:]
        o2 = acc[tk:, 0:LANES] / acc[tk:, LANES:]
        o_ref[r * tk:(r + 1) * tk, :] = o1 - lam * o2


def _attn_prompt(q_bf, kt_bf, v_bf, lams, lam_init, nb, seq):
    tk = min(ATTN_TK, seq)
    tq = min(ATTN_TQ, seq)
    assert tq % tk == 0 and ATTN_GROUPS[-1] == 1
    assert all(a == 2 * b for a, b in zip(ATTN_GROUPS, ATTN_GROUPS[1:]))
    nq, nr = seq // tq, tq // tk
    lam_spec = pl.BlockSpec((1, DA_HDIM), lambda b, h, i: (0, 0))
    return pl.pallas_call(
        functools.partial(_attn_prompt_kernel, tq=tq, tk=tk, lam_init=lam_init),
        out_shape=jax.ShapeDtypeStruct(q_bf.shape, F32),
        grid=(nb, DA_HEADS, nq),
        in_specs=[pl.BlockSpec((tq, LANES), lambda b, h, i: (b * nq + i, h)),
                  pl.BlockSpec((None, LANES, seq), lambda b, h, i: (b, h, 0)),
                  pl.BlockSpec((seq, LANES), lambda b, h, i: (b, h))] + [lam_spec] * 4,
        out_specs=pl.BlockSpec((tq, LANES), lambda b, h, i: (b * nq + i, h)),
        scratch_shapes=[pltpu.VMEM((nr, 2 * tk, LANES), BF16), pltpu.VMEM((nr, 2 * tk, LANES), F32),
                        pltpu.VMEM((nr, 2 * tk, 2 * LANES), F32)],
        compiler_params=_cparams(("arbitrary",) * 3), name="attn_prompt")(q_bf, kt_bf, v_bf, *lams)


def _hgrn_consts(c):
    nl = int(math.log2(c))
    t = np.arange(c)
    lv = np.zeros((nl, c, c), np.float32)
    msk = np.zeros((nl + 1, c, c), np.float32)
    for l in range(nl):
        b = 1 << l
        mid = (t // (2 * b)) * 2 * b + b
        upper = t >= mid
        for i in range(c):
            if upper[i]:
                lv[l, i, mid[i]:i + 1] = 1.0
            else:
                lv[l, i, i + 1:mid[i]] = 1.0
        same = (t[:, None] // (2 * b)) == (t[None, :] // (2 * b))
        msk[l] = same & upper[:, None] & (~upper)[None, :]
    msk[nl] = np.eye(c)
    tri = np.tril(np.ones((c, c), np.float32))
    rest = np.triu(np.ones((c, c), np.float32), 1)
    d = np.concatenate([lv.reshape(nl * c, c), tri, rest], axis=0)
    return jnp.asarray(d, BF16), jnp.asarray(msk, F32), nl


def _hgrn_kernel(*refs, c, nl, nc, has_s0):
    if has_s0:
        hq_ref, hk_ref, hv_ref, lg_ref, d_ref, msk_ref, s0_ref, o_ref, sout_ref, st_ref = refs
    else:
        hq_ref, hk_ref, hv_ref, lg_ref, d_ref, msk_ref, o_ref, sout_ref, st_ref = refs
    jb = pl.program_id(1)
    heads = range(HG_HEADS)

    @pl.when(jb == 0)
    def _():
        for h in heads:
            st_ref[h] = s0_ref[h].T if has_s0 else jnp.zeros((HG_DIM, HG_DIM), F32)

    def chunk(ci, carry):
        r0 = pl.multiple_of(ci * c, c)
        sls = [(pl.ds(r0, c), slice(h * HG_DIM, (h + 1) * HG_DIM)) for h in heads]
        q = [hq_ref[sl] for sl in sls]
        k = [hk_ref[sl] for sl in sls]
        vb = [hv_ref[sl].astype(BF16) for sl in sls]
        st = [st_ref[h] for h in heads]
        e = []
        for sl in sls:
            g = lg_ref[sl]
            g_hi = g.astype(BF16)
            g_lo = (g - g_hi.astype(F32)).astype(BF16)
            e2 = _dot(d_ref[...], jnp.concatenate([g_hi, g_lo], axis=1))
            e.append(e2[:, 0:HG_DIM] + e2[:, HG_DIM:])
        a = [_dot_nt(q[h].astype(BF16), k[h].astype(BF16)) * msk_ref[nl] for h in heads]
        for l in range(nl):
            for h in heads:
                w = jnp.exp(e[h][l * c:(l + 1) * c])
                a[h] = a[h] + _dot_nt((q[h] * w).astype(BF16), (k[h] * w).astype(BF16)) * msk_ref[l]
        for h in heads:
            gcum = e[h][nl * c:(nl + 1) * c]
            grest = e[h][(nl + 1) * c:(nl + 2) * c]
            o_ref[sls[h]] = (_dot(a[h].astype(BF16), vb[h])
                             + _dot_nt((q[h] * jnp.exp(gcum)).astype(BF16), st[h].astype(BF16)))
            khat = (k[h] * jnp.exp(grest)).astype(BF16)
            st_ref[h] = st[h] * jnp.exp(gcum[c - 1:c, :]) + _dot_tn(vb[h], khat)
        return carry

    lax.fori_loop(0, nc, chunk, 0, unroll=min(nc, HG_UNROLL))

    @pl.when(jb == pl.num_programs(1) - 1)
    def _():
        for h in heads:
            sout_ref[h] = st_ref[h].T


def _hgrn(hq, hk, hv, lg, s0, nb, seq, c, tb):
    d, msk, nl = _hgrn_consts(c)
    nblk = seq // tb
    row = pl.BlockSpec((tb, HG_WIDTH), lambda b, j: (b * nblk + j, 0))
    st_spec = pl.BlockSpec((None, HG_HEADS, HG_DIM, HG_DIM), lambda b, j: (b, 0, 0, 0))
    in_specs = [row] * 4 + [pl.BlockSpec(d.shape, lambda b, j: (0, 0)),
                            pl.BlockSpec(msk.shape, lambda b, j: (0, 0, 0))]
    args = [hq, hk, hv, lg, d, msk]
    if s0 is not None:
        in_specs.append(st_spec)
        args.append(s0)
    return pl.pallas_call(
        functools.partial(_hgrn_kernel, c=c, nl=nl, nc=tb // c, has_s0=s0 is not None),
        out_shape=[jax.ShapeDtypeStruct(hq.shape, F32),
                   jax.ShapeDtypeStruct((nb, HG_HEADS, HG_DIM, HG_DIM), F32)],
        grid=(nb, nblk), in_specs=in_specs, out_specs=[row, st_spec],
        scratch_shapes=[pltpu.VMEM((HG_HEADS, HG_DIM, HG_DIM), F32)],
        compiler_params=_cparams(("arbitrary", "arbitrary")),
        name="hgrn_sample" if s0 is not None else "hgrn_prompt")(*args)


def _channel_kernel(*refs, lam_init, ff_chunk, last):
    (ao_ref, ho_ref, hg_ref, x_ref, pe_ref, sg_ref, ng_ref, wo_ref, n2_ref, wu_ref, wd_ref,
     pn_ref, wg_ref, wp_ref) = refs[:14]
    parts = []
    for h in range(DA_HEADS):
        a = ao_ref[:, h * LANES:(h + 1) * LANES]
        parts.append(_rms(a, SUBLN_EPS) * sg_ref[...] * (1.0 - lam_init))
    for h in range(HG_HEADS):
        sl = slice(h * HG_DIM, (h + 1) * HG_DIM)
        gate = hg_ref[:, sl]
        parts.append(_rms(ho_ref[:, sl], NORM_EPS) * ng_ref[...] * (gate * _sigmoid(gate)))
    x = x_ref[...] + _dot(jnp.concatenate(parts, axis=1).astype(BF16), wo_ref[...])
    hn = (_rms(x, NORM_EPS) * n2_ref[...]).astype(BF16)
    acc = x
    for c0 in range(0, wu_ref.shape[1], ff_chunk):
        u = jnp.maximum(_dot(hn, wu_ref[:, c0:c0 + ff_chunk]), 0.0)
        acc = acc + _dot((u * u).astype(BF16), wd_ref[c0:c0 + ff_chunk, :])
    gate = _sigmoid(_dot((_rms(acc, NORM_EPS) * pn_ref[...]).astype(BF16), wg_ref[...]))
    y = acc + gate * _dot(pe_ref[...].astype(BF16), wp_ref[...])
    if last:
        fn_ref, o_ref = refs[14:]
        o_ref[...] = _rms(y, NORM_EPS) * fn_ref[...]
    else:
        refs[14][...] = y


def _resident(shape):
    return pl.BlockSpec(shape, lambda i: (0,) * len(shape), pipeline_mode=pl.Buffered(1))


def _channel(ao, ho, hg, x2d, pe, sg_row, ng_row, wo_bf, n2_row, wu_bf, wd_bf, pn_row, wg_bf, wp_bf,
             fn_row, lam_init, tm):
    n, d = x2d.shape
    row = lambda i: (i, 0)
    half = pl.BlockSpec((tm, DA_WIDTH), row)
    args = [ao, ho, hg, x2d, pe, sg_row, ng_row, wo_bf, n2_row, wu_bf, wd_bf, pn_row, wg_bf, wp_bf]
    in_specs = [half, half, half, pl.BlockSpec((tm, d), row), pl.BlockSpec((tm, pe.shape[1]), row)]
    in_specs += [_resident(a.shape) for a in args[5:]]
    if fn_row is not None:
        in_specs.append(_resident(fn_row.shape))
        args.append(fn_row)
    return pl.pallas_call(
        functools.partial(_channel_kernel, lam_init=lam_init, ff_chunk=d, last=fn_row is not None),
        out_shape=jax.ShapeDtypeStruct((n, d), F32), grid=(n // tm,),
        in_specs=in_specs, out_specs=pl.BlockSpec((tm, d), row),
        compiler_params=_cparams(("arbitrary",)), name="channel_mix")(*args)


def _attn_sample_kernel(pt_ref, qx_ref, ktn_ref, vn_ref, *refs, pp, page, lam_init):
    kt_refs = refs[:pp]
    v_refs = refs[pp:2 * pp]
    lq1, lk1, lq2, lk2, o_ref, m_ref, l_ref, acc_ref = refs[2 * pp:]
    j = pl.program_id(1)
    rows = qx_ref.shape[0]
    qx = qx_ref[...]
    rhead = (lax.broadcasted_iota(jnp.int32, (rows, LANES), 0) // SAMPLE_TPAD) % DA_HEADS

    def update(kts, vs, mask):
        s = jnp.concatenate([_dot(qx, kt[...].astype(BF16)) for kt in kts], axis=1)
        if mask is not None:
            s = jnp.where(mask, s, MASK_VALUE)
        m_prev = m_ref[...]
        m_next = jnp.maximum(m_prev, jnp.max(s, axis=1, keepdims=True))
        alpha = jnp.exp2(m_prev - m_next)
        p = jnp.exp2(s - jnp.concatenate([m_next] * len(kts), axis=1))
        l_ref[...] = l_ref[...] * alpha + jnp.sum(p, axis=1, keepdims=True)
        pb = p.astype(BF16)
        acc = acc_ref[...] * alpha
        for i, v_ref in enumerate(vs):
            pi = pb[:, i * page:(i + 1) * page]
            for h in range(DA_HEADS):
                vh = v_ref[pl.ds(h, page, stride=DA_HEADS), :].astype(BF16)
                acc = acc + jnp.where(rhead == h, _dot(pi, vh), 0.0)
        acc_ref[...] = acc
        m_ref[...] = m_next

    @pl.when(j == 0)
    def _():
        m_ref[...] = jnp.full(m_ref.shape, MASK_VALUE, F32)
        l_ref[...] = jnp.zeros(l_ref.shape, F32)
        acc_ref[...] = jnp.zeros(acc_ref.shape, F32)
        tok = lax.broadcasted_iota(jnp.int32, (rows, page), 0) % SAMPLE_TPAD
        col = lax.broadcasted_iota(jnp.int32, (rows, page), 1)
        update([ktn_ref], [vn_ref], col <= tok)

    update(kt_refs, v_refs, None)

    @pl.when(j == pl.num_programs(1) - 1)
    def _():
        o = acc_ref[...] / l_ref[...]
        lam = _lam_value(lq1, lk1, lq2, lk2, lam_init)
        half = rows // 2
        for h in range(DA_HEADS):
            pos = o[h * SAMPLE_TPAD:(h + 1) * SAMPLE_TPAD]
            neg = o[half + h * SAMPLE_TPAD:half + (h + 1) * SAMPLE_TPAD]
            o_ref[:, h * LANES:(h + 1) * LANES] = pos - lam * neg


def _attn_sample(page_table, qx, ktn, vn, cache_kt, cache_v2, lams, lam_init, layer):
    nbatch, n_pages = page_table.shape
    page = cache_kt.shape[-1]
    pp = PAGES_PER_STEP
    while n_pages % pp:
        pp //= 2
    rows = qx.shape[1]
    pt_flat = page_table.reshape(-1)

    def page_map(i):
        return lambda b, j, pt: (layer, pt[b * n_pages + j * pp + i], 0, 0)

    per_b = lambda b, j, pt: (b, 0, 0)
    lam_spec = pl.BlockSpec((1, DA_HDIM), lambda b, j, pt: (0, 0))
    kt_specs = [pl.BlockSpec((None, None, DA_WIDTH, page), page_map(i)) for i in range(pp)]
    v_specs = [pl.BlockSpec((None, None, DA_HEADS * page, LANES), page_map(i)) for i in range(pp)]
    grid_spec = pltpu.PrefetchScalarGridSpec(
        num_scalar_prefetch=1, grid=(nbatch, n_pages // pp),
        in_specs=[pl.BlockSpec((None, rows, DA_WIDTH), per_b),
                  pl.BlockSpec((None, DA_WIDTH, page), per_b),
                  pl.BlockSpec((None, DA_HEADS * page, LANES), per_b)] + kt_specs + v_specs + [lam_spec] * 4,
        out_specs=pl.BlockSpec((None, SAMPLE_TPAD, DA_WIDTH), per_b),
        scratch_shapes=[pltpu.VMEM((rows, LANES), F32), pltpu.VMEM((rows, LANES), F32),
                        pltpu.VMEM((rows, LANES), F32)])
    return pl.pallas_call(
        functools.partial(_attn_sample_kernel, pp=pp, page=page, lam_init=lam_init),
        out_shape=jax.ShapeDtypeStruct((nbatch, SAMPLE_TPAD, DA_WIDTH), F32),
        grid_spec=grid_spec, compiler_params=_cparams(("arbitrary", "arbitrary")),
        name="attn_sample")(pt_flat, qx, ktn, vn, *([cache_kt] * pp), *([cache_v2] * pp), *lams)


def _row_tile(n, want):
    t = min(n, want)
    while n % t:
        t //= 2
    return t


def kernel(x_prompt, x_sample, cache_k, cache_v, state_h, page_table, p_prompt, p_sample,
           norm1_g, w_in, lam_q1, lam_k1, lam_q2, lam_k2, subln_g, hg_lb_logits, hg_norm_g,
           w_out, norm2_g, w_up, w_down, ple_norm_g, w_ple_gate, w_ple_proj, final_norm_g):
    nb, seq, d_model = x_prompt.shape
    ns, t_new, _ = x_sample.shape
    depth = w_in.shape[0]
    n_pool, page = cache_k.shape[1], cache_k.shape[2]
    past_len = page_table.shape[1] * page
    n_p, n_s = nb * seq, ns * t_new
    assert w_in.shape[2] == N_GROUPS * DA_WIDTH and t_new <= SAMPLE_TPAD <= page
    assert seq % HG_CHUNK == 0 and t_new <= HG_CHUNK_SAMPLE

    tm_p = _row_tile(seq, 512)
    tm_s = n_s
    tb_h = _row_tile(seq, 4 * HG_CHUNK)
    tables_p = _rope_tables(jnp.arange(seq, dtype=jnp.int32))
    tables_s = _rope_tables(jnp.tile(past_len + jnp.arange(t_new, dtype=jnp.int32), ns))

    cache_kt = jnp.transpose(cache_k, (0, 1, 3, 4, 2)).reshape(depth, n_pool, DA_WIDTH, page)
    cache_v2 = cache_v.reshape(depth, n_pool, page * DA_HEADS, LANES)

    lane_map = np.arange(DA_WIDTH) // DA_HDIM
    qmask = np.zeros((2, DA_HEADS, 1, DA_WIDTH), np.float32)
    for sgn in range(2):
        for h in range(DA_HEADS):
            qmask[sgn, h, 0] = lane_map == 2 * h + sgn
    qmask = jnp.asarray(qmask, BF16)

    xp = x_prompt.reshape(n_p, d_model)
    xs = x_sample.reshape(n_s, d_model)
    kt_all = v_all = None
    hp_l, ks_l, vs_l, hs_l = [], [], [], []
    row1 = lambda a: a.reshape(1, -1)
    for i in range(depth):
        lam_init = 0.8 - 0.6 * math.exp(-0.3 * i)
        lams = [row1(lam_q1[i]), row1(lam_k1[i]), row1(lam_q2[i]), row1(lam_k2[i])]
        w_bf = w_in[i].astype(BF16)
        wkt_bf = w_in[i][:, DA_WIDTH:2 * DA_WIDTH].T.astype(BF16)
        wo_bf, wu_bf, wd_bf = w_out[i].astype(BF16), w_up[i].astype(BF16), w_down[i].astype(BF16)
        wg_bf, wp_bf = w_ple_gate[i].astype(BF16), w_ple_proj[i].astype(BF16)
        g1, sg, ng = row1(norm1_g[i]), row1(subln_g[i]), row1(hg_norm_g[i])
        n2, pn = row1(norm2_g[i]), row1(ple_norm_g[i])
        fn = row1(final_norm_g) if i == depth - 1 else None

        (q_bf, kt_all, kt_bf, v_all, v_bf, hq, hk, lg, hi, hg) = _proj_in(
            xp, g1, w_bf, wkt_bf, tables_p, hg_lb_logits, i, tm_p, seq,
            kt_prev=kt_all, v_prev=v_all, depth=depth, k_transposed=True)
        ao = _attn_prompt(q_bf, kt_bf, v_bf, lams, lam_init, nb, seq)
        ho, hstate = _hgrn(hq, hk, hi, lg, None, nb, seq, HG_CHUNK, tb_h)
        xp = _channel(ao, ho, hg, xp, p_prompt[i].reshape(n_p, -1), sg, ng, wo_bf, n2, wu_bf, wd_bf,
                      pn, wg_bf, wp_bf, fn, lam_init, tm_p)
        hp_l.append(hstate)

        (q_bf, k_s, k_bf, v_s, v_bf, hq, hk, lg, hi, hg) = _proj_in(
            xs, g1, w_bf, None, tables_s, hg_lb_logits, i, tm_s, n_s, k_transposed=False)
        qpad = jnp.pad(q_bf.reshape(ns, 1, 1, t_new, DA_WIDTH),
                       ((0, 0), (0, 0), (0, 0), (0, SAMPLE_TPAD - t_new), (0, 0)))
        qx = (qpad * qmask).reshape(ns, 2 * DA_HEADS * SAMPLE_TPAD, DA_WIDTH)
        ktn = jnp.pad(jnp.swapaxes(k_bf.reshape(ns, t_new, DA_WIDTH), 1, 2),
                      ((0, 0), (0, 0), (0, page - t_new)))
        vn = jnp.pad(v_s.reshape(ns, t_new, DA_WIDTH), ((0, 0), (0, page - t_new), (0, 0)))
        vn = vn.reshape(ns, page * DA_HEADS, LANES)
        ao8 = _attn_sample(page_table, qx, ktn, vn, cache_kt, cache_v2, lams, lam_init, i)
        ao = ao8[:, :t_new].reshape(n_s, DA_WIDTH)
        padc = lambda a: jnp.pad(a.reshape(ns, t_new, HG_WIDTH),
                                 ((0, 0), (0, HG_CHUNK_SAMPLE - t_new), (0, 0))).reshape(-1, HG_WIDTH)
        ho16, hstate = _hgrn(padc(hq), padc(hk), padc(hi), padc(lg), state_h[i], ns,
                             HG_CHUNK_SAMPLE, HG_CHUNK_SAMPLE, HG_CHUNK_SAMPLE)
        ho = ho16.reshape(ns, HG_CHUNK_SAMPLE, HG_WIDTH)[:, :t_new].reshape(n_s, HG_WIDTH)
        xs = _channel(ao, ho, hg, xs, p_sample[i].reshape(n_s, -1), sg, ng, wo_bf, n2, wu_bf, wd_bf,
                      pn, wg_bf, wp_bf, fn, lam_init, tm_s)
        ks_l.append(k_s.reshape(ns, t_new, 2 * DA_HEADS, DA_HDIM))
        vs_l.append(v_s.reshape(ns, t_new, DA_HEADS, 2 * DA_HDIM))
        hs_l.append(hstate)

    k_prompt = jnp.transpose(kt_all.reshape(depth, nb, 2 * DA_HEADS, DA_HDIM, seq), (0, 1, 4, 2, 3))
    v_prompt = v_all.reshape(depth, nb, seq, DA_HEADS, 2 * DA_HDIM)
    return (xp.reshape(nb, seq, d_model), xs.reshape(ns, t_new, d_model), k_prompt, v_prompt,
            jnp.stack(hp_l), jnp.stack(ks_l), jnp.stack(vs_l), jnp.stack(hs_l))
```

```python
import functools
import math

import numpy as np
import jax
import jax.numpy as jnp
from jax import lax
from jax.experimental import pallas as pl
from jax.experimental.pallas import tpu as pltpu

F32 = jnp.float32
BF16 = jnp.bfloat16

DA_HEADS = 4
DA_HDIM = 64
DA_WIDTH = 2 * DA_HEADS * DA_HDIM
HG_HEADS = 4
HG_DIM = 128
HG_WIDTH = HG_HEADS * HG_DIM
N_GROUPS = 7
ROPE_THETA = 500000.0
ROT_DIM = DA_HDIM // 4
ROT_HALF = ROT_DIM // 2
NORM_EPS = 1e-6
SUBLN_EPS = 1e-5
MASK_VALUE = -1e30
QK_SCALE_LOG2 = DA_HDIM ** -0.5 * math.log2(math.e)
LANES = 128
VMEM_LIMIT = 56 * 1024 * 1024

HG_CHUNK = 64
HG_UNROLL = 4
HG_CHUNK_SAMPLE = 16
ATTN_TQ = 2048
ATTN_TK = 256
ATTN_GROUPS = (8, 4, 2, 1)
PAGES_PER_STEP = 32
SAMPLE_TPAD = 8


def _cparams(sem):
    return pltpu.CompilerParams(dimension_semantics=sem, vmem_limit_bytes=VMEM_LIMIT)


def _dot(a, b):
    return jnp.dot(a, b, preferred_element_type=F32)


def _dot_nt(a, b):
    return lax.dot_general(a, b, (((1,), (1,)), ((), ())), preferred_element_type=F32)


def _dot_tn(a, b):
    return lax.dot_general(a, b, (((0,), (0,)), ((), ())), preferred_element_type=F32)


def _rms(x, eps):
    return x * lax.rsqrt(jnp.mean(x * x, axis=-1, keepdims=True) + eps)


def _sigmoid(x):
    return 1.0 / (1.0 + jnp.exp(-x))


def _rope_rows(z, c_ref, s1_ref, s2_ref):
    c, s1, s2 = c_ref[...], s1_ref[...], s2_ref[...]
    outs = []
    for j in range(DA_WIDTH // LANES):
        zc = z[:, j * LANES:(j + 1) * LANES]
        up = pltpu.roll(zc, LANES - ROT_HALF, 1)
        dn = pltpu.roll(zc, ROT_HALF, 1)
        outs.append(zc * c + up * s1 + dn * s2)
    return jnp.concatenate(outs, axis=1)


def _proj_in_kernel(*refs, layer, k_transposed):
    if k_transposed:
        (x_ref, g_ref, w_ref, wkt_ref, c_ref, s1_ref, s2_ref, ct_ref, st_ref, lbl_ref,
         _kt_alias, _v_alias,
         q_ref, kt_ref, ktb_ref, v_ref, vb_ref, hq_ref, hk_ref, lg_ref, hi_ref, hg_ref) = refs
    else:
        (x_ref, g_ref, w_ref, c_ref, s1_ref, s2_ref, lbl_ref,
         q_ref, k_ref, kb_ref, v_ref, vb_ref, hq_ref, hk_ref, lg_ref, hi_ref, hg_ref) = refs
    xb = (_rms(x_ref[...], NORM_EPS) * g_ref[...]).astype(BF16)

    def cols(g):
        return _dot(xb, w_ref[:, g * DA_WIDTH:(g + 1) * DA_WIDTH])

    q_ref[...] = (_rope_rows(cols(0), c_ref, s1_ref, s2_ref) * QK_SCALE_LOG2).astype(BF16)
    if k_transposed:
        zk = _dot_nt(wkt_ref[...], xb)
        kt_ref[...] = zk
        ktb_ref[...] = zk.astype(BF16)
        ct, st = ct_ref[...], st_ref[...]
        for j in range(2 * DA_HEADS):
            r = j * DA_HDIM
            x1 = zk[r:r + ROT_HALF]
            x2 = zk[r + ROT_HALF:r + ROT_DIM]
            o1 = x1 * ct - x2 * st
            o2 = x2 * ct + x1 * st
            kt_ref[r:r + ROT_HALF, :] = o1
            kt_ref[r + ROT_HALF:r + ROT_DIM, :] = o2
            ktb_ref[r:r + ROT_DIM, :] = jnp.concatenate([o1, o2], axis=0).astype(BF16)
    else:
        k = _rope_rows(cols(1), c_ref, s1_ref, s2_ref)
        k_ref[...] = k
        kb_ref[...] = k.astype(BF16)
    v = cols(2)
    if k_transposed:
        for h in range(DA_HEADS):
            v_ref[:, h, :] = v[:, h * LANES:(h + 1) * LANES]
    else:
        v_ref[...] = v
    vb_ref[...] = v.astype(BF16)
    zq = cols(3)
    hq_ref[...] = zq * _sigmoid(zq)
    lbl = lbl_ref[...]
    e = jnp.exp(lbl - jnp.max(lbl, axis=0, keepdims=True))
    sm = e / jnp.sum(e, axis=0, keepdims=True)
    lb = jnp.zeros_like(sm[0:1])
    for l in range(1, layer + 1):
        lb = lb + sm[l:l + 1]
    hk = (1.0 - lb) * _sigmoid(-cols(4))
    hk_ref[...] = hk
    lg_ref[...] = jnp.log1p(-hk)
    hi_ref[...] = cols(5)
    hg_ref[...] = cols(6)


def _rope_tables(pos):
    inv = ROPE_THETA ** (-(jnp.arange(ROT_HALF, dtype=F32) * 2.0) / ROT_DIM)
    ang = pos.astype(F32)[:, None] * inv[None, :]
    cos, sin = jnp.cos(ang), jnp.sin(ang)
    d = np.arange(LANES) % DA_HDIM
    reps = LANES // ROT_HALF
    cosl, sinl = jnp.tile(cos, (1, reps)), jnp.tile(sin, (1, reps))
    c = jnp.where((d < ROT_DIM)[None, :], cosl, 1.0)
    s1 = jnp.where((d < ROT_HALF)[None, :], -sinl, 0.0)
    s2 = jnp.where(((d >= ROT_HALF) & (d < ROT_DIM))[None, :], sinl, 0.0)
    return c, s1, s2, cos.T, sin.T


def _proj_in(x2d, g_row, w_bf, wkt_bf, tables, lbl, layer, tm, period, kt_prev=None, v_prev=None,
             depth=None, k_transposed=True):
    n, d = x2d.shape
    npt = period // tm
    nb = n // period
    c, s1, s2, ct, st = tables
    row = lambda i: (i, 0)
    const = lambda i: (0, 0)
    tab = lambda i: (i % npt, 0)
    in_specs = [pl.BlockSpec((tm, d), row), pl.BlockSpec((1, d), const),
                pl.BlockSpec(w_bf.shape, const)]
    args = [x2d, g_row, w_bf]
    if k_transposed:
        in_specs.append(pl.BlockSpec(wkt_bf.shape, const))
        args.append(wkt_bf)
    in_specs += [pl.BlockSpec((tm, LANES), tab)] * 3
    args += [c, s1, s2]
    if k_transposed:
        in_specs += [pl.BlockSpec((ROT_HALF, tm), lambda i: (0, i % npt))] * 2
        args += [ct, st]
    in_specs.append(pl.BlockSpec(lbl.shape, const))
    args.append(lbl)
    act = jax.ShapeDtypeStruct((n, DA_WIDTH), F32)
    act_b = jax.ShapeDtypeStruct((n, DA_WIDTH), BF16)
    act_spec = pl.BlockSpec((tm, DA_WIDTH), row)
    aliases = {}
    if k_transposed:
        kt_shape = jax.ShapeDtypeStruct((depth, nb, DA_WIDTH, period), F32)
        v_shape = jax.ShapeDtypeStruct((depth, n, DA_HEADS, LANES), F32)
        if kt_prev is not None:
            aliases = {len(args): 1, len(args) + 1: 3}
            in_specs += [pl.BlockSpec(memory_space=pl.ANY)] * 2
            args += [kt_prev, v_prev]
        else:
            in_specs += [pl.BlockSpec((1, d), const)] * 2
            args += [g_row, g_row]
        out_shape = [act_b, kt_shape, jax.ShapeDtypeStruct((nb, DA_WIDTH, period), BF16),
                     v_shape, act_b] + [act] * 5
        out_specs = [act_spec,
                     pl.BlockSpec((None, None, DA_WIDTH, tm), lambda i: (layer, i // npt, 0, i % npt)),
                     pl.BlockSpec((None, DA_WIDTH, tm), lambda i: (i // npt, 0, i % npt)),
                     pl.BlockSpec((None, tm, DA_HEADS, LANES), lambda i: (layer, i, 0, 0)),
                     act_spec] + [act_spec] * 5
    else:
        out_shape = [act_b, act, act_b, act, act_b] + [act] * 5
        out_specs = [act_spec] * 10
    return pl.pallas_call(
        functools.partial(_proj_in_kernel, layer=layer, k_transposed=k_transposed),
        out_shape=out_shape, grid=(n // tm,), in_specs=in_specs, out_specs=out_specs,
        input_output_aliases=aliases, compiler_params=_cparams(("arbitrary",)),
        name="proj_in_prompt" if k_transposed else "proj_in_sample")(*args)


def _lam_value(lq1, lk1, lq2, lk2, lam_init):
    a = jnp.sum(lq1[...] * lk1[...], axis=-1, keepdims=True)
    b = jnp.sum(lq2[...] * lk2[...], axis=-1, keepdims=True)
    return jnp.exp(a) - jnp.exp(b) + lam_init


def _attn_prompt_kernel(q_ref, kt_ref, v_ref, lq1, lk1, lq2, lk2, o_ref, qq_ref, m_ref, acc_ref,
                        *, tq, tk, lam_init):
    qi = pl.program_id(2)
    nr = tq // tk
    q = q_ref[...]
    lane = lax.broadcasted_iota(jnp.int32, (tk, LANES), 1)
    for r in range(nr):
        qr = q[r * tk:(r + 1) * tk]
        zero = jnp.zeros_like(qr)
        qq_ref[r, 0:tk, :] = jnp.where(lane < DA_HDIM, qr, zero)
        qq_ref[r, tk:2 * tk, :] = jnp.where(lane >= DA_HDIM, qr, zero)
    m_ref[...] = jnp.full(m_ref.shape, MASK_VALUE, F32)
    acc_ref[...] = jnp.zeros(acc_ref.shape, F32)
    ones = jnp.ones((tk, LANES), BF16)

    def block(kb, full, diag):
        off = pl.multiple_of(kb * tk, tk)
        kt = kt_ref[:, pl.ds(off, tk)]
        vx = jnp.concatenate([v_ref[pl.ds(off, tk), :], ones], axis=1)
        for r in list(full) + ([diag] if diag is not None else []):
            s = _dot(qq_ref[r], kt)
            if r == diag:
                row = lax.broadcasted_iota(jnp.int32, s.shape, 0) % tk
                col = lax.broadcasted_iota(jnp.int32, s.shape, 1)
                s = jnp.where(col <= row, s, MASK_VALUE)
            m_prev = m_ref[r]
            m_next = jnp.maximum(m_prev, jnp.max(s, axis=1, keepdims=True))
            alpha = jnp.exp2(m_prev - m_next)
            p = jnp.exp2(s - jnp.concatenate([m_next] * (tk // LANES), axis=1)).astype(BF16)
            acc_ref[r] = acc_ref[r] * jnp.concatenate([alpha, alpha], axis=1) + _dot(p, vx)
            m_ref[r] = m_next

    everyone = range(nr)
    nfull = qi * nr

    def run(base, count):
        for u in range(count):
            block(base + u, everyone, None)

    sizes = [g for g in ATTN_GROUPS if g >= nr] or [nr]
    top = sizes[0]
    ntop = nfull // top

    def top_group(gi, carry):
        run(gi * top, top)
        return carry

    lax.fori_loop(0, ntop, top_group, 0)
    done = ntop * top
    for size in sizes[1:]:
        take = ((nfull - done) // size) % 2 if size > 1 else (nfull - done)
        base = done

        @pl.when(take > 0)
        def _(base=base, size=size):
            run(base, size)

        done = done + take * size
    for r in range(nr):
        block(nfull + r, range(r + 1, nr), r)
    lam = _lam_value(lq1, lk1, lq2, lk2, lam_init)
    for r in range(nr):
        acc = acc_ref[r]
        o1 = acc[0:tk, 0:LANES] / acc[0:tk, LANES:]
        o2 = acc[tk:, 0:LANES] / acc[tk:, LANES:]
        o_ref[r * tk:(r + 1) * tk, :] = o1 - lam * o2


def _attn_prompt(q_bf, kt_bf, v_bf, lams, lam_init, nb, seq):
    tk = min(ATTN_TK, seq)
    tq = min(ATTN_TQ, seq)
    assert tq % tk == 0 and ATTN_GROUPS[-1] == 1
    assert all(a == 2 * b for a, b in zip(ATTN_GROUPS, ATTN_GROUPS[1:]))
    nq, nr = seq // tq, tq // tk
    lam_spec = pl.BlockSpec((1, DA_HDIM), lambda b, h, i: (0, 0))
    return pl.pallas_call(
        functools.partial(_attn_prompt_kernel, tq=tq, tk=tk, lam_init=lam_init),
        out_shape=jax.ShapeDtypeStruct(q_bf.shape, F32),
        grid=(nb, DA_HEADS, nq),
        in_specs=[pl.BlockSpec((tq, LANES), lambda b, h, i: (b * nq + i, h)),
                  pl.BlockSpec((None, LANES, seq), lambda b, h, i: (b, h, 0)),
                  pl.BlockSpec((seq, LANES), lambda b, h, i: (b, h))] + [lam_spec] * 4,
        out_specs=pl.BlockSpec((tq, LANES), lambda b, h, i: (b * nq + i, h)),
        scratch_shapes=[pltpu.VMEM((nr, 2 * tk, LANES), BF16), pltpu.VMEM((nr, 2 * tk, LANES), F32),
                        pltpu.VMEM((nr, 2 * tk, 2 * LANES), F32)],
        compiler_params=_cparams(("arbitrary",) * 3), name="attn_prompt")(q_bf, kt_bf, v_bf, *lams)


def _hgrn_consts(c):
    nl = int(math.log2(c))
    t = np.arange(c)
    lv = np.zeros((nl, c, c), np.float32)
    msk = np.zeros((nl + 1, c, c), np.float32)
    for l in range(nl):
        b = 1 << l
        mid = (t // (2 * b)) * 2 * b + b
        upper = t >= mid
        for i in range(c):
            if upper[i]:
                lv[l, i, mid[i]:i + 1] = 1.0
            else:
                lv[l, i, i + 1:mid[i]] = 1.0
        same = (t[:, None] // (2 * b)) == (t[None, :] // (2 * b))
        msk[l] = same & upper[:, None] & (~upper)[None, :]
    msk[nl] = np.eye(c)
    tri = np.tril(np.ones((c, c), np.float32))
    rest = np.triu(np.ones((c, c), np.float32), 1)
    d = np.concatenate([lv.reshape(nl * c, c), tri, rest], axis=0)
    return jnp.asarray(d, BF16), jnp.asarray(msk, F32), nl


def _hgrn_kernel(*refs, c, nl, nc, has_s0):
    if has_s0:
        hq_ref, hk_ref, hv_ref, lg_ref, d_ref, msk_ref, s0_ref, o_ref, sout_ref, st_ref = refs
    else:
        hq_ref, hk_ref, hv_ref, lg_ref, d_ref, msk_ref, o_ref, sout_ref, st_ref = refs
    jb = pl.program_id(1)
    heads = range(HG_HEADS)

    @pl.when(jb == 0)
    def _():
        for h in heads:
            st_ref[h] = s0_ref[h].T if has_s0 else jnp.zeros((HG_DIM, HG_DIM), F32)

    def chunk(ci, carry):
        r0 = pl.multiple_of(ci * c, c)
        sls = [(pl.ds(r0, c), slice(h * HG_DIM, (h + 1) * HG_DIM)) for h in heads]
        q = [hq_ref[sl] for sl in sls]
        k = [hk_ref[sl] for sl in sls]
        vb = [hv_ref[sl].astype(BF16) for sl in sls]
        st = [st_ref[h] for h in heads]
        e = []
        for sl in sls:
            g = lg_ref[sl]
            g_hi = g.astype(BF16)
            g_lo = (g - g_hi.astype(F32)).astype(BF16)
            e2 = _dot(d_ref[...], jnp.concatenate([g_hi, g_lo], axis=1))
            e.append(e2[:, 0:HG_DIM] + e2[:, HG_DIM:])
        a = [_dot_nt(q[h].astype(BF16), k[h].astype(BF16)) * msk_ref[nl] for h in heads]
        for l in range(nl):
            for h in heads:
                w = jnp.exp(e[h][l * c:(l + 1) * c])
                a[h] = a[h] + _dot_nt((q[h] * w).astype(BF16), (k[h] * w).astype(BF16)) * msk_ref[l]
        for h in heads:
            gcum = e[h][nl * c:(nl + 1) * c]
            grest = e[h][(nl + 1) * c:(nl + 2) * c]
            o_ref[sls[h]] = (_dot(a[h].astype(BF16), vb[h])
                             + _dot_nt((q[h] * jnp.exp(gcum)).astype(BF16), st[h].astype(BF16)))
            khat = (k[h] * jnp.exp(grest)).astype(BF16)
            st_ref[h] = st[h] * jnp.exp(gcum[c - 1:c, :]) + _dot_tn(vb[h], khat)
        return carry

    lax.fori_loop(0, nc, chunk, 0, unroll=min(nc, HG_UNROLL))

    @pl.when(jb == pl.num_programs(1) - 1)
    def _():
        for h in heads:
            sout_ref[h] = st_ref[h].T


def _hgrn(hq, hk, hv, lg, s0, nb, seq, c, tb):
    d, msk, nl = _hgrn_consts(c)
    nblk = seq // tb
    row = pl.BlockSpec((tb, HG_WIDTH), lambda b, j: (b * nblk + j, 0))
    st_spec = pl.BlockSpec((None, HG_HEADS, HG_DIM, HG_DIM), lambda b, j: (b, 0, 0, 0))
    in_specs = [row] * 4 + [pl.BlockSpec(d.shape, lambda b, j: (0, 0)),
                            pl.BlockSpec(msk.shape, lambda b, j: (0, 0, 0))]
    args = [hq, hk, hv, lg, d, msk]
    if s0 is not None:
        in_specs.append(st_spec)
        args.append(s0)
    return pl.pallas_call(
        functools.partial(_hgrn_kernel, c=c, nl=nl, nc=tb // c, has_s0=s0 is not None),
        out_shape=[jax.ShapeDtypeStruct(hq.shape, F32),
                   jax.ShapeDtypeStruct((nb, HG_HEADS, HG_DIM, HG_DIM), F32)],
        grid=(nb, nblk), in_specs=in_specs, out_specs=[row, st_spec],
        scratch_shapes=[pltpu.VMEM((HG_HEADS, HG_DIM, HG_DIM), F32)],
        compiler_params=_cparams(("arbitrary", "arbitrary")),
        name="hgrn_sample" if s0 is not None else "hgrn_prompt")(*args)


def _channel_kernel(*refs, lam_init, ff_chunk, last):
    (ao_ref, ho_ref, hg_ref, x_ref, pe_ref, sg_ref, ng_ref, wo_ref, n2_ref, wu_ref, wd_ref,
     pn_ref, wg_ref, wp_ref) = refs[:14]
    parts = []
    for h in range(DA_HEADS):
        a = ao_ref[:, h * LANES:(h + 1) * LANES]
        parts.append(_rms(a, SUBLN_EPS) * sg_ref[...] * (1.0 - lam_init))
    for h in range(HG_HEADS):
        sl = slice(h * HG_DIM, (h + 1) * HG_DIM)
        gate = hg_ref[:, sl]
        parts.append(_rms(ho_ref[:, sl], NORM_EPS) * ng_ref[...] * (gate * _sigmoid(gate)))
    x = x_ref[...] + _dot(jnp.concatenate(parts, axis=1).astype(BF16), wo_ref[...])
    hn = (_rms(x, NORM_EPS) * n2_ref[...]).astype(BF16)
    acc = x
    for c0 in range(0, wu_ref.shape[1], ff_chunk):
        u = jnp.maximum(_dot(hn, wu_ref[:, c0:c0 + ff_chunk]), 0.0)
        acc = acc + _dot((u * u).astype(BF16), wd_ref[c0:c0 + ff_chunk, :])
    gate = _sigmoid(_dot((_rms(acc, NORM_EPS) * pn_ref[...]).astype(BF16), wg_ref[...]))
    y = acc + gate * _dot(pe_ref[...].astype(BF16), wp_ref[...])
    if last:
        fn_ref, o_ref = refs[14:]
        o_ref[...] = _rms(y, NORM_EPS) * fn_ref[...]
    else:
        refs[14][...] = y


def _resident(shape):
    return pl.BlockSpec(shape, lambda i: (0,) * len(shape), pipeline_mode=pl.Buffered(1))


def _channel(ao, ho, hg, x2d, pe, sg_row, ng_row, wo_bf, n2_row, wu_bf, wd_bf, pn_row, wg_bf, wp_bf,
             fn_row, lam_init, tm):
    n, d = x2d.shape
    row = lambda i: (i, 0)
    half = pl.BlockSpec((tm, DA_WIDTH), row)
    args = [ao, ho, hg, x2d, pe, sg_row, ng_row, wo_bf, n2_row, wu_bf, wd_bf, pn_row, wg_bf, wp_bf]
    in_specs = [half, half, half, pl.BlockSpec((tm, d), row), pl.BlockSpec((tm, pe.shape[1]), row)]
    in_specs += [_resident(a.shape) for a in args[5:]]
    if fn_row is not None:
        in_specs.append(_resident(fn_row.shape))
        args.append(fn_row)
    return pl.pallas_call(
        functools.partial(_channel_kernel, lam_init=lam_init, ff_chunk=d, last=fn_row is not None),
        out_shape=jax.ShapeDtypeStruct((n, d), F32), grid=(n // tm,),
        in_specs=in_specs, out_specs=pl.BlockSpec((tm, d), row),
        compiler_params=_cparams(("arbitrary",)), name="channel_mix")(*args)


def _attn_sample_kernel(pt_ref, qx_ref, ktn_ref, vn_ref, *refs, pp, page, lam_init):
    kt_refs = refs[:pp]
    v_refs = refs[pp:2 * pp]
    lq1, lk1, lq2, lk2, o_ref, m_ref, l_ref, acc_ref = refs[2 * pp:]
    j = pl.program_id(1)
    rows = qx_ref.shape[0]
    qx = qx_ref[...]
    rhead = (lax.broadcasted_iota(jnp.int32, (rows, LANES), 0) // SAMPLE_TPAD) % DA_HEADS

    def update(kts, vs, mask):
        s = jnp.concatenate([_dot(qx, kt[...].astype(BF16)) for kt in kts], axis=1)
        if mask is not None:
            s = jnp.where(mask, s, MASK_VALUE)
        m_prev = m_ref[...]
        m_next = jnp.maximum(m_prev, jnp.max(s, axis=1, keepdims=True))
        alpha = jnp.exp2(m_prev - m_next)
        p = jnp.exp2(s - jnp.concatenate([m_next] * len(kts), axis=1))
        l_ref[...] = l_ref[...] * alpha + jnp.sum(p, axis=1, keepdims=True)
        pb = p.astype(BF16)
        acc = acc_ref[...] * alpha
        for i, v_ref in enumerate(vs):
            pi = pb[:, i * page:(i + 1) * page]
            for h in range(DA_HEADS):
                vh = v_ref[pl.ds(h, page, stride=DA_HEADS), :].astype(BF16)
                acc = acc + jnp.where(rhead == h, _dot(pi, vh), 0.0)
        acc_ref[...] = acc
        m_ref[...] = m_next

    @pl.when(j == 0)
    def _():
        m_ref[...] = jnp.full(m_ref.shape, MASK_VALUE, F32)
        l_ref[...] = jnp.zeros(l_ref.shape, F32)
        acc_ref[...] = jnp.zeros(acc_ref.shape, F32)
        tok = lax.broadcasted_iota(jnp.int32, (rows, page), 0) % SAMPLE_TPAD
        col = lax.broadcasted_iota(jnp.int32, (rows, page), 1)
        update([ktn_ref], [vn_ref], col <= tok)

    update(kt_refs, v_refs, None)

    @pl.when(j == pl.num_programs(1) - 1)
    def _():
        o = acc_ref[...] / l_ref[...]
        lam = _lam_value(lq1, lk1, lq2, lk2, lam_init)
        half = rows // 2
        for h in range(DA_HEADS):
            pos = o[h * SAMPLE_TPAD:(h + 1) * SAMPLE_TPAD]
            neg = o[half + h * SAMPLE_TPAD:half + (h + 1) * SAMPLE_TPAD]
            o_ref[:, h * LANES:(h + 1) * LANES] = pos - lam * neg


def _attn_sample(page_table, qx, ktn, vn, cache_kt, cache_v2, lams, lam_init, layer):
    nbatch, n_pages = page_table.shape
    page = cache_kt.shape[-1]
    pp = PAGES_PER_STEP
    while n_pages % pp:
        pp //= 2
    rows = qx.shape[1]
    pt_flat = page_table.reshape(-1)

    def page_map(i):
        return lambda b, j, pt: (layer, pt[b * n_pages + j * pp + i], 0, 0)

    per_b = lambda b, j, pt: (b, 0, 0)
    lam_spec = pl.BlockSpec((1, DA_HDIM), lambda b, j, pt: (0, 0))
    kt_specs = [pl.BlockSpec((None, None, DA_WIDTH, page), page_map(i)) for i in range(pp)]
    v_specs = [pl.BlockSpec((None, None, DA_HEADS * page, LANES), page_map(i)) for i in range(pp)]
    grid_spec = pltpu.PrefetchScalarGridSpec(
        num_scalar_prefetch=1, grid=(nbatch, n_pages // pp),
        in_specs=[pl.BlockSpec((None, rows, DA_WIDTH), per_b),
                  pl.BlockSpec((None, DA_WIDTH, page), per_b),
                  pl.BlockSpec((None, DA_HEADS * page, LANES), per_b)] + kt_specs + v_specs + [lam_spec] * 4,
        out_specs=pl.BlockSpec((None, SAMPLE_TPAD, DA_WIDTH), per_b),
        scratch_shapes=[pltpu.VMEM((rows, LANES), F32), pltpu.VMEM((rows, LANES), F32),
                        pltpu.VMEM((rows, LANES), F32)])
    return pl.pallas_call(
        functools.partial(_attn_sample_kernel, pp=pp, page=page, lam_init=lam_init),
        out_shape=jax.ShapeDtypeStruct((nbatch, SAMPLE_TPAD, DA_WIDTH), F32),
        grid_spec=grid_spec, compiler_params=_cparams(("arbitrary", "arbitrary")),
        name="attn_sample")(pt_flat, qx, ktn, vn, *([cache_kt] * pp), *([cache_v2] * pp), *lams)


def _row_tile(n, want):
    t = min(n, want)
    while n % t:
        t //= 2
    return t


def kernel(x_prompt, x_sample, cache_k, cache_v, state_h, page_table, p_prompt, p_sample,
           norm1_g, w_in, lam_q1, lam_k1, lam_q2, lam_k2, subln_g, hg_lb_logits, hg_norm_g,
           w_out, norm2_g, w_up, w_down, ple_norm_g, w_ple_gate, w_ple_proj, final_norm_g):
    nb, seq, d_model = x_prompt.shape
    ns, t_new, _ = x_sample.shape
    depth = w_in.shape[0]
    n_pool, page = cache_k.shape[1], cache_k.shape[2]
    past_len = page_table.shape[1] * page
    n_p, n_s = nb * seq, ns * t_new
    assert w_in.shape[2] == N_GROUPS * DA_WIDTH and t_new <= SAMPLE_TPAD <= page
    assert seq % HG_CHUNK == 0 and t_new <= HG_CHUNK_SAMPLE

    tm_p = _row_tile(seq, 512)
    tm_s = n_s
    tb_h = _row_tile(seq, 4 * HG_CHUNK)
    tables_p = _rope_tables(jnp.arange(seq, dtype=jnp.int32))
    tables_s = _rope_tables(jnp.tile(past_len + jnp.arange(t_new, dtype=jnp.int32), ns))

    cache_kt = jnp.transpose(cache_k, (0, 1, 3, 4, 2)).reshape(depth, n_pool, DA_WIDTH, page)
    cache_v2 = cache_v.reshape(depth, n_pool, page * DA_HEADS, LANES)

    lane_map = np.arange(DA_WIDTH) // DA_HDIM
    qmask = np.zeros((2, DA_HEADS, 1, DA_WIDTH), np.float32)
    for sgn in range(2):
        for h in range(DA_HEADS):
            qmask[sgn, h, 0] = lane_map == 2 * h + sgn
    qmask = jnp.asarray(qmask, BF16)

    xp = x_prompt.reshape(n_p, d_model)
    xs = x_sample.reshape(n_s, d_model)
    kt_all = v_all = None
    hp_l, ks_l, vs_l, hs_l = [], [], [], []
    row1 = lambda a: a.reshape(1, -1)
    for i in range(depth):
        lam_init = 0.8 - 0.6 * math.exp(-0.3 * i)
        lams = [row1(lam_q1[i]), row1(lam_k1[i]), row1(lam_q2[i]), row1(lam_k2[i])]
        w_bf = w_in[i].astype(BF16)
        wkt_bf = w_in[i][:, DA_WIDTH:2 * DA_WIDTH].T.astype(BF16)
        wo_bf, wu_bf, wd_bf = w_out[i].astype(BF16), w_up[i].astype(BF16), w_down[i].astype(BF16)
        wg_bf, wp_bf = w_ple_gate[i].astype(BF16), w_ple_proj[i].astype(BF16)
        g1, sg, ng = row1(norm1_g[i]), row1(subln_g[i]), row1(hg_norm_g[i])
        n2, pn = row1(norm2_g[i]), row1(ple_norm_g[i])
        fn = row1(final_norm_g) if i == depth - 1 else None

        (q_bf, kt_all, kt_bf, v_all, v_bf, hq, hk, lg, hi, hg) = _proj_in(
            xp, g1, w_bf, wkt_bf, tables_p, hg_lb_logits, i, tm_p, seq,
            kt_prev=kt_all, v_prev=v_all, depth=depth, k_transposed=True)
        ao = _attn_prompt(q_bf, kt_bf, v_bf, lams, lam_init, nb, seq)
        ho, hstate = _hgrn(hq, hk, hi, lg, None, nb, seq, HG_CHUNK, tb_h)
        xp = _channel(ao, ho, hg, xp, p_prompt[i].reshape(n_p, -1), sg, ng, wo_bf, n2, wu_bf, wd_bf,
                      pn, wg_bf, wp_bf, fn, lam_init, tm_p)
        hp_l.append(hstate)

        (q_bf, k_s, k_bf, v_s, v_bf, hq, hk, lg, hi, hg) = _proj_in(
            xs, g1, w_bf, None, tables_s, hg_lb_logits, i, tm_s, n_s, k_transposed=False)
        qpad = jnp.pad(q_bf.reshape(ns, 1, 1, t_new, DA_WIDTH),
                       ((0, 0), (0, 0), (0, 0), (0, SAMPLE_TPAD - t_new), (0, 0)))
        qx = (qpad * qmask).reshape(ns, 2 * DA_HEADS * SAMPLE_TPAD, DA_WIDTH)
        ktn = jnp.pad(jnp.swapaxes(k_bf.reshape(ns, t_new, DA_WIDTH), 1, 2),
                      ((0, 0), (0, 0), (0, page - t_new)))
        vn = jnp.pad(v_s.reshape(ns, t_new, DA_WIDTH), ((0, 0), (0, page - t_new), (0, 0)))
        vn = vn.reshape(ns, page * DA_HEADS, LANES)
        ao8 = _attn_sample(page_table, qx, ktn, vn, cache_kt, cache_v2, lams, lam_init, i)
        ao = ao8[:, :t_new].reshape(n_s, DA_WIDTH)
        padc = lambda a: jnp.pad(a.reshape(ns, t_new, HG_WIDTH),
                                 ((0, 0), (0, HG_CHUNK_SAMPLE - t_new), (0, 0))).reshape(-1, HG_WIDTH)
        ho16, hstate = _hgrn(padc(hq), padc(hk), padc(hi), padc(lg), state_h[i], ns,
                             HG_CHUNK_SAMPLE, HG_CHUNK_SAMPLE, HG_CHUNK_SAMPLE)
        ho = ho16.reshape(ns, HG_CHUNK_SAMPLE, HG_WIDTH)[:, :t_new].reshape(n_s, HG_WIDTH)
        xs = _channel(ao, ho, hg, xs, p_sample[i].reshape(n_s, -1), sg, ng, wo_bf, n2, wu_bf, wd_bf,
                      pn, wg_bf, wp_bf, fn, lam_init, tm_s)
        ks_l.append(k_s.reshape(ns, t_new, 2 * DA_HEADS, DA_HDIM))
        vs_l.append(v_s.reshape(ns, t_new, DA_HEADS, 2 * DA_HDIM))
        hs_l.append(hstate)

    k_prompt = jnp.transpose(kt_all.reshape(depth, nb, 2 * DA_HEADS, DA_HDIM, seq), (0, 1, 4, 2, 3))
    v_prompt = v_all.reshape(depth, nb, seq, DA_HEADS, 2 * DA_HDIM)
    return (xp.reshape(nb, seq, d_model), xs.reshape(ns, t_new, d_model), k_prompt, v_prompt,
            jnp.stack(hp_l), jnp.stack(ks_l), jnp.stack(vs_l), jnp.stack(hs_l))
```

```python
import functools
import math

import numpy as np
import jax
import jax.numpy as jnp
from jax import lax
from jax.experimental import pallas as pl
from jax.experimental.pallas import tpu as pltpu

F32 = jnp.float32
BF16 = jnp.bfloat16

DA_HEADS = 4
DA_HDIM = 64
DA_WIDTH = 2 * DA_HEADS * DA_HDIM
HG_HEADS = 4
HG_DIM = 128
HG_WIDTH = HG_HEADS * HG_DIM
N_GROUPS = 7
ROPE_THETA = 500000.0
ROT_DIM = DA_HDIM // 4
ROT_HALF = ROT_DIM // 2
NORM_EPS = 1e-6
SUBLN_EPS = 1e-5
MASK_VALUE = -1e30
QK_SCALE_LOG2 = DA_HDIM ** -0.5 * math.log2(math.e)
LANES = 128
VMEM_LIMIT = 56 * 1024 * 1024

HG_CHUNK = 64
HG_MATMUL_LEVELS = 3
HG_UNROLL = 4
HG_CHUNK_SAMPLE = 16
ATTN_TQ = 2048
ATTN_TK = 256
ATTN_GROUPS = (8, 4, 2, 1)
PAGES_PER_STEP = 32
SAMPLE_TPAD = 8


def _cparams(sem):
    return pltpu.CompilerParams(dimension_semantics=sem, vmem_limit_bytes=VMEM_LIMIT)


def _dot(a, b):
    return jnp.dot(a, b, preferred_element_type=F32)


def _dot_nt(a, b):
    return lax.dot_general(a, b, (((1,), (1,)), ((), ())), preferred_element_type=F32)


def _dot_tn(a, b):
    return lax.dot_general(a, b, (((0,), (0,)), ((), ())), preferred_element_type=F32)


def _rms(x, eps):
    return x * lax.rsqrt(jnp.mean(x * x, axis=-1, keepdims=True) + eps)


def _sigmoid(x):
    return 1.0 / (1.0 + jnp.exp(-x))


def _rope_rows(z, c_ref, s1_ref, s2_ref):
    c, s1, s2 = c_ref[...], s1_ref[...], s2_ref[...]
    outs = []
    for j in range(DA_WIDTH // LANES):
        zc = z[:, j * LANES:(j + 1) * LANES]
        up = pltpu.roll(zc, LANES - ROT_HALF, 1)
        dn = pltpu.roll(zc, ROT_HALF, 1)
        outs.append(zc * c + up * s1 + dn * s2)
    return jnp.concatenate(outs, axis=1)


def _proj_in_kernel(*refs, layer, k_transposed):
    if k_transposed:
        (x_ref, g_ref, w_ref, wkt_ref, c_ref, s1_ref, s2_ref, ct_ref, st_ref, lbl_ref,
         _kt_alias, _v_alias,
         q_ref, kt_ref, ktb_ref, v_ref, vb_ref, hq_ref, hk_ref, lg_ref, hi_ref, hg_ref) = refs
    else:
        (x_ref, g_ref, w_ref, c_ref, s1_ref, s2_ref, lbl_ref,
         q_ref, k_ref, kb_ref, v_ref, vb_ref, hq_ref, hk_ref, lg_ref, hi_ref, hg_ref) = refs
    xb = (_rms(x_ref[...], NORM_EPS) * g_ref[...]).astype(BF16)

    def cols(g):
        return _dot(xb, w_ref[:, g * DA_WIDTH:(g + 1) * DA_WIDTH])

    q_ref[...] = (_rope_rows(cols(0), c_ref, s1_ref, s2_ref) * QK_SCALE_LOG2).astype(BF16)
    if k_transposed:
        zk = _dot_nt(wkt_ref[...], xb)
        kt_ref[...] = zk
        ktb_ref[...] = zk.astype(BF16)
        ct, st = ct_ref[...], st_ref[...]
        for j in range(2 * DA_HEADS):
            r = j * DA_HDIM
            x1 = zk[r:r + ROT_HALF]
            x2 = zk[r + ROT_HALF:r + ROT_DIM]
            o1 = x1 * ct - x2 * st
            o2 = x2 * ct + x1 * st
            kt_ref[r:r + ROT_HALF, :] = o1
            kt_ref[r + ROT_HALF:r + ROT_DIM, :] = o2
            ktb_ref[r:r + ROT_DIM, :] = jnp.concatenate([o1, o2], axis=0).astype(BF16)
    else:
        k = _rope_rows(cols(1), c_ref, s1_ref, s2_ref)
        k_ref[...] = k
        kb_ref[...] = k.astype(BF16)
    v = cols(2)
    if k_transposed:
        for h in range(DA_HEADS):
            v_ref[:, h, :] = v[:, h * LANES:(h + 1) * LANES]
    else:
        v_ref[...] = v
    vb_ref[...] = v.astype(BF16)
    zq = cols(3)
    hq_ref[...] = zq * _sigmoid(zq)
    lbl = lbl_ref[...]
    e = jnp.exp(lbl - jnp.max(lbl, axis=0, keepdims=True))
    sm = e / jnp.sum(e, axis=0, keepdims=True)
    lb = jnp.zeros_like(sm[0:1])
    for l in range(1, layer + 1):
        lb = lb + sm[l:l + 1]
    hk = (1.0 - lb) * _sigmoid(-cols(4))
    hk_ref[...] = hk
    lg_ref[...] = jnp.log1p(-hk)
    hi_ref[...] = cols(5)
    hg_ref[...] = cols(6)


def _rope_tables(pos):
    inv = ROPE_THETA ** (-(jnp.arange(ROT_HALF, dtype=F32) * 2.0) / ROT_DIM)
    ang = pos.astype(F32)[:, None] * inv[None, :]
    cos, sin = jnp.cos(ang), jnp.sin(ang)
    d = np.arange(LANES) % DA_HDIM
    reps = LANES // ROT_HALF
    cosl, sinl = jnp.tile(cos, (1, reps)), jnp.tile(sin, (1, reps))
    c = jnp.where((d < ROT_DIM)[None, :], cosl, 1.0)
    s1 = jnp.where((d < ROT_HALF)[None, :], -sinl, 0.0)
    s2 = jnp.where(((d >= ROT_HALF) & (d < ROT_DIM))[None, :], sinl, 0.0)
    return c, s1, s2, cos.T, sin.T


def _proj_in(x2d, g_row, w_bf, wkt_bf, tables, lbl, layer, tm, period, kt_prev=None, v_prev=None,
             depth=None, k_transposed=True):
    n, d = x2d.shape
    npt = period // tm
    nb = n // period
    c, s1, s2, ct, st = tables
    row = lambda i: (i, 0)
    const = lambda i: (0, 0)
    tab = lambda i: (i % npt, 0)
    in_specs = [pl.BlockSpec((tm, d), row), pl.BlockSpec((1, d), const),
                pl.BlockSpec(w_bf.shape, const)]
    args = [x2d, g_row, w_bf]
    if k_transposed:
        in_specs.append(pl.BlockSpec(wkt_bf.shape, const))
        args.append(wkt_bf)
    in_specs += [pl.BlockSpec((tm, LANES), tab)] * 3
    args += [c, s1, s2]
    if k_transposed:
        in_specs += [pl.BlockSpec((ROT_HALF, tm), lambda i: (0, i % npt))] * 2
        args += [ct, st]
    in_specs.append(pl.BlockSpec(lbl.shape, const))
    args.append(lbl)
    act = jax.ShapeDtypeStruct((n, DA_WIDTH), F32)
    act_b = jax.ShapeDtypeStruct((n, DA_WIDTH), BF16)
    act_spec = pl.BlockSpec((tm, DA_WIDTH), row)
    aliases = {}
    if k_transposed:
        kt_shape = jax.ShapeDtypeStruct((depth, nb, DA_WIDTH, period), F32)
        v_shape = jax.ShapeDtypeStruct((depth, n, DA_HEADS, LANES), F32)
        if kt_prev is not None:
            aliases = {len(args): 1, len(args) + 1: 3}
            in_specs += [pl.BlockSpec(memory_space=pl.ANY)] * 2
            args += [kt_prev, v_prev]
        else:
            in_specs += [pl.BlockSpec((1, d), const)] * 2
            args += [g_row, g_row]
        out_shape = [act_b, kt_shape, jax.ShapeDtypeStruct((nb, DA_WIDTH, period), BF16),
                     v_shape, act_b] + [act] * 5
        out_specs = [act_spec,
                     pl.BlockSpec((None, None, DA_WIDTH, tm), lambda i: (layer, i // npt, 0, i % npt)),
                     pl.BlockSpec((None, DA_WIDTH, tm), lambda i: (i // npt, 0, i % npt)),
                     pl.BlockSpec((None, tm, DA_HEADS, LANES), lambda i: (layer, i, 0, 0)),
                     act_spec] + [act_spec] * 5
    else:
        out_shape = [act_b, act, act_b, act, act_b] + [act] * 5
        out_specs = [act_spec] * 10
    return pl.pallas_call(
        functools.partial(_proj_in_kernel, layer=layer, k_transposed=k_transposed),
        out_shape=out_shape, grid=(n // tm,), in_specs=in_specs, out_specs=out_specs,
        input_output_aliases=aliases, compiler_params=_cparams(("arbitrary",)),
        name="proj_in_prompt" if k_transposed else "proj_in_sample")(*args)


def _lam_value(lq1, lk1, lq2, lk2, lam_init):
    a = jnp.sum(lq1[...] * lk1[...], axis=-1, keepdims=True)
    b = jnp.sum(lq2[...] * lk2[...], axis=-1, keepdims=True)
    return jnp.exp(a) - jnp.exp(b) + lam_init


def _attn_prompt_kernel(q_ref, kt_ref, v_ref, lq1, lk1, lq2, lk2, o_ref, qq_ref, m_ref, acc_ref,
                        *, tq, tk, lam_init):
    qi = pl.program_id(2)
    nr = tq // tk
    q = q_ref[...]
    lane = lax.broadcasted_iota(jnp.int32, (tk, LANES), 1)
    for r in range(nr):
        qr = q[r * tk:(r + 1) * tk]
        zero = jnp.zeros_like(qr)
        qq_ref[r, 0:tk, :] = jnp.where(lane < DA_HDIM, qr, zero)
        qq_ref[r, tk:2 * tk, :] = jnp.where(lane >= DA_HDIM, qr, zero)
    m_ref[...] = jnp.full(m_ref.shape, MASK_VALUE, F32)
    acc_ref[...] = jnp.zeros(acc_ref.shape, F32)
    ones = jnp.ones((tk, LANES), BF16)

    def block(kb, full, diag):
        off = pl.multiple_of(kb * tk, tk)
        kt = kt_ref[:, pl.ds(off, tk)]
        vx = jnp.concatenate([v_ref[pl.ds(off, tk), :], ones], axis=1)
        for r in list(full) + ([diag] if diag is not None else []):
            s = _dot(qq_ref[r], kt)
            if r == diag:
                row = lax.broadcasted_iota(jnp.int32, s.shape, 0) % tk
                col = lax.broadcasted_iota(jnp.int32, s.shape, 1)
                s = jnp.where(col <= row, s, MASK_VALUE)
            m_prev = m_ref[r]
            m_next = jnp.maximum(m_prev, jnp.max(s, axis=1, keepdims=True))
            alpha = jnp.exp2(m_prev - m_next)
            p = jnp.exp2(s - jnp.concatenate([m_next] * (tk // LANES), axis=1)).astype(BF16)
            acc_ref[r] = acc_ref[r] * jnp.concatenate([alpha, alpha], axis=1) + _dot(p, vx)
            m_ref[r] = m_next

    everyone = range(nr)
    nfull = qi * nr

    def run(base, count):
        for u in range(count):
            block(base + u, everyone, None)

    sizes = [g for g in ATTN_GROUPS if g >= nr] or [nr]
    top = sizes[0]
    ntop = nfull // top

    def top_group(gi, carry):
        run(gi * top, top)
        return carry

    lax.fori_loop(0, ntop, top_group, 0)
    done = ntop * top
    for size in sizes[1:]:
        take = ((nfull - done) // size) % 2 if size > 1 else (nfull - done)
        base = done

        @pl.when(take > 0)
        def _(base=base, size=size):
            run(base, size)

        done = done + take * size
    for r in range(nr):
        block(nfull + r, range(r + 1, nr), r)
    lam = _lam_value(lq1, lk1, lq2, lk2, lam_init)
    for r in range(nr):
        acc = acc_ref[r]
        o1 = acc[0:tk, 0:LANES] / acc[0:tk, LANES:]
        o2 = acc[tk:, 0:LANES] / acc[tk:, LANES:]
        o_ref[r * tk:(r + 1) * tk, :] = o1 - lam * o2


def _attn_prompt(q_bf, kt_bf, v_bf, lams, lam_init, nb, seq):
    tk = min(ATTN_TK, seq)
    tq = min(ATTN_TQ, seq)
    assert tq % tk == 0 and ATTN_GROUPS[-1] == 1
    assert all(a == 2 * b for a, b in zip(ATTN_GROUPS, ATTN_GROUPS[1:]))
    nq, nr = seq // tq, tq // tk
    lam_spec = pl.BlockSpec((1, DA_HDIM), lambda b, h, i: (0, 0))
    return pl.pallas_call(
        functools.partial(_attn_prompt_kernel, tq=tq, tk=tk, lam_init=lam_init),
        out_shape=jax.ShapeDtypeStruct(q_bf.shape, F32),
        grid=(nb, DA_HEADS, nq),
        in_specs=[pl.BlockSpec((tq, LANES), lambda b, h, i: (b * nq + i, h)),
                  pl.BlockSpec((None, LANES, seq), lambda b, h, i: (b, h, 0)),
                  pl.BlockSpec((seq, LANES), lambda b, h, i: (b, h))] + [lam_spec] * 4,
        out_specs=pl.BlockSpec((tq, LANES), lambda b, h, i: (b * nq + i, h)),
        scratch_shapes=[pltpu.VMEM((nr, 2 * tk, LANES), BF16), pltpu.VMEM((nr, 2 * tk, LANES), F32),
                        pltpu.VMEM((nr, 2 * tk, 2 * LANES), F32)],
        compiler_params=_cparams(("arbitrary",) * 3), name="attn_prompt")(q_bf, kt_bf, v_bf, *lams)


def _hgrn_consts(c):
    nl = int(math.log2(c))
    t = np.arange(c)
    lv = np.zeros((nl, c, c), np.float32)
    msk = np.zeros((nl + 1, c, c), np.float32)
    for l in range(nl):
        b = 1 << l
        mid = (t // (2 * b)) * 2 * b + b
        upper = t >= mid
        for i in range(c):
            if upper[i]:
                lv[l, i, mid[i]:i + 1] = 1.0
            else:
                lv[l, i, i + 1:mid[i]] = 1.0
        same = (t[:, None] // (2 * b)) == (t[None, :] // (2 * b))
        msk[l] = same & upper[:, None] & (~upper)[None, :]
    msk[nl] = np.eye(c)
    tri = np.tril(np.ones((c, c), np.float32))
    nmat = min(nl, HG_MATMUL_LEVELS)
    d = np.concatenate([lv[:nmat].reshape(nmat * c, c), tri], axis=0)
    return jnp.asarray(d, BF16), jnp.asarray(msk, F32), nl, nmat


def _hgrn_kernel(*refs, c, nl, nmat, nc, has_s0):
    if has_s0:
        hq_ref, hk_ref, hv_ref, lg_ref, d_ref, msk_ref, s0_ref, o_ref, sout_ref, st_ref = refs
    else:
        hq_ref, hk_ref, hv_ref, lg_ref, d_ref, msk_ref, o_ref, sout_ref, st_ref = refs
    jb = pl.program_id(1)
    heads = range(HG_HEADS)

    @pl.when(jb == 0)
    def _():
        for h in heads:
            st_ref[h] = s0_ref[h].T if has_s0 else jnp.zeros((HG_DIM, HG_DIM), F32)

    def chunk(ci, carry):
        r0 = pl.multiple_of(ci * c, c)
        sls = [(pl.ds(r0, c), slice(h * HG_DIM, (h + 1) * HG_DIM)) for h in heads]
        q = [hq_ref[sl] for sl in sls]
        k = [hk_ref[sl] for sl in sls]
        vb = [hv_ref[sl].astype(BF16) for sl in sls]
        st = [st_ref[h] for h in heads]
        e = []
        for sl in sls:
            g = lg_ref[sl]
            g_hi = g.astype(BF16)
            g_lo = (g - g_hi.astype(F32)).astype(BF16)
            e2 = _dot(d_ref[...], jnp.concatenate([g_hi, g_lo], axis=1))
            e.append(e2[:, 0:HG_DIM] + e2[:, HG_DIM:])
        gcum = [e[h][nmat * c:(nmat + 1) * c] for h in heads]

        def level_exponent(h, l):
            if l < nmat:
                return e[h][l * c:(l + 1) * c]
            b = 1 << l
            rows = [jnp.broadcast_to(gcum[h][x + b - 1:x + b], (2 * b, HG_DIM)) for x in range(0, c, 2 * b)]
            diff = gcum[h] - (rows[0] if len(rows) == 1 else jnp.concatenate(rows, axis=0))
            return jnp.minimum(diff, -diff)

        a = [_dot_nt(q[h].astype(BF16), k[h].astype(BF16)) * msk_ref[nl] for h in heads]
        for l in range(nl):
            for h in heads:
                w = jnp.exp(level_exponent(h, l))
                a[h] = a[h] + _dot_nt((q[h] * w).astype(BF16), (k[h] * w).astype(BF16)) * msk_ref[l]
        for h in heads:
            glast = gcum[h][c - 1:c, :]
            o_ref[sls[h]] = (_dot(a[h].astype(BF16), vb[h])
                             + _dot_nt((q[h] * jnp.exp(gcum[h])).astype(BF16), st[h].astype(BF16)))
            khat = (k[h] * jnp.exp(glast - gcum[h])).astype(BF16)
            st_ref[h] = st[h] * jnp.exp(glast) + _dot_tn(vb[h], khat)
        return carry

    lax.fori_loop(0, nc, chunk, 0, unroll=min(nc, HG_UNROLL))

    @pl.when(jb == pl.num_programs(1) - 1)
    def _():
        for h in heads:
            sout_ref[h] = st_ref[h].T


def _hgrn(hq, hk, hv, lg, s0, nb, seq, c, tb):
    d, msk, nl, nmat = _hgrn_consts(c)
    nblk = seq // tb
    row = pl.BlockSpec((tb, HG_WIDTH), lambda b, j: (b * nblk + j, 0))
    st_spec = pl.BlockSpec((None, HG_HEADS, HG_DIM, HG_DIM), lambda b, j: (b, 0, 0, 0))
    in_specs = [row] * 4 + [pl.BlockSpec(d.shape, lambda b, j: (0, 0)),
                            pl.BlockSpec(msk.shape, lambda b, j: (0, 0, 0))]
    args = [hq, hk, hv, lg, d, msk]
    if s0 is not None:
        in_specs.append(st_spec)
        args.append(s0)
    return pl.pallas_call(
        functools.partial(_hgrn_kernel, c=c, nl=nl, nmat=nmat, nc=tb // c, has_s0=s0 is not None),
        out_shape=[jax.ShapeDtypeStruct(hq.shape, F32),
                   jax.ShapeDtypeStruct((nb, HG_HEADS, HG_DIM, HG_DIM), F32)],
        grid=(nb, nblk), in_specs=in_specs, out_specs=[row, st_spec],
        scratch_shapes=[pltpu.VMEM((HG_HEADS, HG_DIM, HG_DIM), F32)],
        compiler_params=_cparams(("arbitrary", "arbitrary")),
        name="hgrn_sample" if s0 is not None else "hgrn_prompt")(*args)


def _channel_kernel(*refs, lam_init, ff_chunk, last):
    (ao_ref, ho_ref, hg_ref, x_ref, pe_ref, sg_ref, ng_ref, wo_ref, n2_ref, wu_ref, wd_ref,
     pn_ref, wg_ref, wp_ref) = refs[:14]
    parts = []
    for h in range(DA_HEADS):
        a = ao_ref[:, h * LANES:(h + 1) * LANES]
        parts.append(_rms(a, SUBLN_EPS) * sg_ref[...] * (1.0 - lam_init))
    for h in range(HG_HEADS):
        sl = slice(h * HG_DIM, (h + 1) * HG_DIM)
        gate = hg_ref[:, sl]
        parts.append(_rms(ho_ref[:, sl], NORM_EPS) * ng_ref[...] * (gate * _sigmoid(gate)))
    x = x_ref[...] + _dot(jnp.concatenate(parts, axis=1).astype(BF16), wo_ref[...])
    hn = (_rms(x, NORM_EPS) * n2_ref[...]).astype(BF16)
    acc = x
    for c0 in range(0, wu_ref.shape[1], ff_chunk):
        u = jnp.maximum(_dot(hn, wu_ref[:, c0:c0 + ff_chunk]), 0.0)
        acc = acc + _dot((u * u).astype(BF16), wd_ref[c0:c0 + ff_chunk, :])
    gate = _sigmoid(_dot((_rms(acc, NORM_EPS) * pn_ref[...]).astype(BF16), wg_ref[...]))
    y = acc + gate * _dot(pe_ref[...].astype(BF16), wp_ref[...])
    if last:
        fn_ref, o_ref = refs[14:]
        o_ref[...] = _rms(y, NORM_EPS) * fn_ref[...]
    else:
        refs[14][...] = y


def _resident(shape):
    return pl.BlockSpec(shape, lambda i: (0,) * len(shape), pipeline_mode=pl.Buffered(1))


def _channel(ao, ho, hg, x2d, pe, sg_row, ng_row, wo_bf, n2_row, wu_bf, wd_bf, pn_row, wg_bf, wp_bf,
             fn_row, lam_init, layer, tm):
    n, d = x2d.shape
    row = lambda i: (i, 0)
    half = pl.BlockSpec((tm, DA_WIDTH), row)
    args = [ao, ho, hg, x2d, pe, sg_row, ng_row, wo_bf, n2_row, wu_bf, wd_bf, pn_row, wg_bf, wp_bf]
    in_specs = [half, half, half, pl.BlockSpec((tm, d), row),
                pl.BlockSpec((None, tm, pe.shape[2]), lambda i: (layer, i, 0))]
    in_specs += [_resident(a.shape) for a in args[5:]]
    if fn_row is not None:
        in_specs.append(_resident(fn_row.shape))
        args.append(fn_row)
    return pl.pallas_call(
        functools.partial(_channel_kernel, lam_init=lam_init, ff_chunk=d, last=fn_row is not None),
        out_shape=jax.ShapeDtypeStruct((n, d), F32), grid=(n // tm,),
        in_specs=in_specs, out_specs=pl.BlockSpec((tm, d), row),
        compiler_params=_cparams(("arbitrary",)), name="channel_mix")(*args)


def _attn_sample_kernel(pt_ref, qx_ref, ktn_ref, vn_ref, *refs, pp, page, lam_init):
    kt_refs = refs[:pp]
    v_refs = refs[pp:2 * pp]
    lq1, lk1, lq2, lk2, o_ref, m_ref, l_ref, acc_ref = refs[2 * pp:]
    j = pl.program_id(1)
    rows = qx_ref.shape[0]
    qx = qx_ref[...]
    rhead = (lax.broadcasted_iota(jnp.int32, (rows, LANES), 0) // SAMPLE_TPAD) % DA_HEADS

    def update(kts, vs, mask):
        s = jnp.concatenate([_dot(qx, kt[...].astype(BF16)) for kt in kts], axis=1)
        if mask is not None:
            s = jnp.where(mask, s, MASK_VALUE)
        m_prev = m_ref[...]
        m_next = jnp.maximum(m_prev, jnp.max(s, axis=1, keepdims=True))
        alpha = jnp.exp2(m_prev - m_next)
        p = jnp.exp2(s - jnp.concatenate([m_next] * len(kts), axis=1))
        l_ref[...] = l_ref[...] * alpha + jnp.sum(p, axis=1, keepdims=True)
        pb = p.astype(BF16)
        acc = acc_ref[...] * alpha
        for i, v_ref in enumerate(vs):
            pi = pb[:, i * page:(i + 1) * page]
            for h in range(DA_HEADS):
                vh = v_ref[pl.ds(h, page, stride=DA_HEADS), :].astype(BF16)
                acc = acc + jnp.where(rhead == h, _dot(pi, vh), 0.0)
        acc_ref[...] = acc
        m_ref[...] = m_next

    @pl.when(j == 0)
    def _():
        m_ref[...] = jnp.full(m_ref.shape, MASK_VALUE, F32)
        l_ref[...] = jnp.zeros(l_ref.shape, F32)
        acc_ref[...] = jnp.zeros(acc_ref.shape, F32)
        tok = lax.broadcasted_iota(jnp.int32, (rows, page), 0) % SAMPLE_TPAD
        col = lax.broadcasted_iota(jnp.int32, (rows, page), 1)
        update([ktn_ref], [vn_ref], col <= tok)

    update(kt_refs, v_refs, None)

    @pl.when(j == pl.num_programs(1) - 1)
    def _():
        o = acc_ref[...] / l_ref[...]
        lam = _lam_value(lq1, lk1, lq2, lk2, lam_init)
        half = rows // 2
        for h in range(DA_HEADS):
            pos = o[h * SAMPLE_TPAD:(h + 1) * SAMPLE_TPAD]
            neg = o[half + h * SAMPLE_TPAD:half + (h + 1) * SAMPLE_TPAD]
            o_ref[:, h * LANES:(h + 1) * LANES] = pos - lam * neg


def _attn_sample(page_table, qx, ktn, vn, cache_kt, cache_v2, lams, lam_init, layer):
    nbatch, n_pages = page_table.shape
    page = cache_kt.shape[-1]
    pp = PAGES_PER_STEP
    while n_pages % pp:
        pp //= 2
    rows = qx.shape[1]
    pt_flat = page_table.reshape(-1)

    def page_map(i):
        return lambda b, j, pt: (layer, pt[b * n_pages + j * pp + i], 0, 0)

    per_b = lambda b, j, pt: (b, 0, 0)
    lam_spec = pl.BlockSpec((1, DA_HDIM), lambda b, j, pt: (0, 0))
    kt_specs = [pl.BlockSpec((None, None, DA_WIDTH, page), page_map(i)) for i in range(pp)]
    v_specs = [pl.BlockSpec((None, None, DA_HEADS * page, LANES), page_map(i)) for i in range(pp)]
    grid_spec = pltpu.PrefetchScalarGridSpec(
        num_scalar_prefetch=1, grid=(nbatch, n_pages // pp),
        in_specs=[pl.BlockSpec((None, rows, DA_WIDTH), per_b),
                  pl.BlockSpec((None, DA_WIDTH, page), per_b),
                  pl.BlockSpec((None, DA_HEADS * page, LANES), per_b)] + kt_specs + v_specs + [lam_spec] * 4,
        out_specs=pl.BlockSpec((None, SAMPLE_TPAD, DA_WIDTH), per_b),
        scratch_shapes=[pltpu.VMEM((rows, LANES), F32), pltpu.VMEM((rows, LANES), F32),
                        pltpu.VMEM((rows, LANES), F32)])
    return pl.pallas_call(
        functools.partial(_attn_sample_kernel, pp=pp, page=page, lam_init=lam_init),
        out_shape=jax.ShapeDtypeStruct((nbatch, SAMPLE_TPAD, DA_WIDTH), F32),
        grid_spec=grid_spec, compiler_params=_cparams(("arbitrary", "arbitrary")),
        name="attn_sample")(pt_flat, qx, ktn, vn, *([cache_kt] * pp), *([cache_v2] * pp), *lams)


def _row_tile(n, want):
    t = min(n, want)
    while n % t:
        t //= 2
    return t


def kernel(x_prompt, x_sample, cache_k, cache_v, state_h, page_table, p_prompt, p_sample,
           norm1_g, w_in, lam_q1, lam_k1, lam_q2, lam_k2, subln_g, hg_lb_logits, hg_norm_g,
           w_out, norm2_g, w_up, w_down, ple_norm_g, w_ple_gate, w_ple_proj, final_norm_g):
    nb, seq, d_model = x_prompt.shape
    ns, t_new, _ = x_sample.shape
    depth = w_in.shape[0]
    n_pool, page = cache_k.shape[1], cache_k.shape[2]
    past_len = page_table.shape[1] * page
    n_p, n_s = nb * seq, ns * t_new
    assert w_in.shape[2] == N_GROUPS * DA_WIDTH and t_new <= SAMPLE_TPAD <= page
    assert seq % HG_CHUNK == 0 and t_new <= HG_CHUNK_SAMPLE

    tm_p = _row_tile(seq, 512)
    tm_s = n_s
    tb_h = _row_tile(seq, 4 * HG_CHUNK)
    tables_p = _rope_tables(jnp.arange(seq, dtype=jnp.int32))
    tables_s = _rope_tables(jnp.tile(past_len + jnp.arange(t_new, dtype=jnp.int32), ns))

    cache_kt = jnp.transpose(cache_k, (0, 1, 3, 4, 2)).reshape(depth, n_pool, DA_WIDTH, page)
    cache_v2 = cache_v.reshape(depth, n_pool, page * DA_HEADS, LANES)

    lane_map = np.arange(DA_WIDTH) // DA_HDIM
    qmask = np.zeros((2, DA_HEADS, 1, DA_WIDTH), np.float32)
    for sgn in range(2):
        for h in range(DA_HEADS):
            qmask[sgn, h, 0] = lane_map == 2 * h + sgn
    qmask = jnp.asarray(qmask, BF16)

    xp = x_prompt.reshape(n_p, d_model)
    xs = x_sample.reshape(n_s, d_model)
    kt_all = v_all = None
    hp_l, ks_l, vs_l, hs_l = [], [], [], []
    row1 = lambda a: a.reshape(1, -1)
    for i in range(depth):
        lam_init = 0.8 - 0.6 * math.exp(-0.3 * i)
        lams = [row1(lam_q1[i]), row1(lam_k1[i]), row1(lam_q2[i]), row1(lam_k2[i])]
        w_bf = w_in[i].astype(BF16)
        wkt_bf = w_in[i][:, DA_WIDTH:2 * DA_WIDTH].T.astype(BF16)
        wo_bf, wu_bf, wd_bf = w_out[i].astype(BF16), w_up[i].astype(BF16), w_down[i].astype(BF16)
        wg_bf, wp_bf = w_ple_gate[i].astype(BF16), w_ple_proj[i].astype(BF16)
        g1, sg, ng = row1(norm1_g[i]), row1(subln_g[i]), row1(hg_norm_g[i])
        n2, pn = row1(norm2_g[i]), row1(ple_norm_g[i])
        fn = row1(final_norm_g) if i == depth - 1 else None

        (q_bf, kt_all, kt_bf, v_all, v_bf, hq, hk, lg, hi, hg) = _proj_in(
            xp, g1, w_bf, wkt_bf, tables_p, hg_lb_logits, i, tm_p, seq,
            kt_prev=kt_all, v_prev=v_all, depth=depth, k_transposed=True)
        ao = _attn_prompt(q_bf, kt_bf, v_bf, lams, lam_init, nb, seq)
        ho, hstate = _hgrn(hq, hk, hi, lg, None, nb, seq, HG_CHUNK, tb_h)
        xp = _channel(ao, ho, hg, xp, p_prompt.reshape(depth, n_p, -1), sg, ng, wo_bf, n2, wu_bf, wd_bf,
                      pn, wg_bf, wp_bf, fn, lam_init, i, tm_p)
        hp_l.append(hstate)

        (q_bf, k_s, k_bf, v_s, v_bf, hq, hk, lg, hi, hg) = _proj_in(
            xs, g1, w_bf, None, tables_s, hg_lb_logits, i, tm_s, n_s, k_transposed=False)
        qpad = jnp.pad(q_bf.reshape(ns, 1, 1, t_new, DA_WIDTH),
                       ((0, 0), (0, 0), (0, 0), (0, SAMPLE_TPAD - t_new), (0, 0)))
        qx = (qpad * qmask).reshape(ns, 2 * DA_HEADS * SAMPLE_TPAD, DA_WIDTH)
        ktn = jnp.pad(jnp.swapaxes(k_bf.reshape(ns, t_new, DA_WIDTH), 1, 2),
                      ((0, 0), (0, 0), (0, page - t_new)))
        vn = jnp.pad(v_s.reshape(ns, t_new, DA_WIDTH), ((0, 0), (0, page - t_new), (0, 0)))
        vn = vn.reshape(ns, page * DA_HEADS, LANES)
        ao8 = _attn_sample(page_table, qx, ktn, vn, cache_kt, cache_v2, lams, lam_init, i)
        ao = ao8[:, :t_new].reshape(n_s, DA_WIDTH)
        padc = lambda a: jnp.pad(a.reshape(ns, t_new, HG_WIDTH),
                                 ((0, 0), (0, HG_CHUNK_SAMPLE - t_new), (0, 0))).reshape(-1, HG_WIDTH)
        ho16, hstate = _hgrn(padc(hq), padc(hk), padc(hi), padc(lg), state_h[i], ns,
                             HG_CHUNK_SAMPLE, HG_CHUNK_SAMPLE, HG_CHUNK_SAMPLE)
        ho = ho16.reshape(ns, HG_CHUNK_SAMPLE, HG_WIDTH)[:, :t_new].reshape(n_s, HG_WIDTH)
        xs = _channel(ao, ho, hg, xs, p_sample.reshape(depth, n_s, -1), sg, ng, wo_bf, n2, wu_bf, wd_bf,
                      pn, wg_bf, wp_bf, fn, lam_init, i, tm_s)
        ks_l.append(k_s.reshape(ns, t_new, 2 * DA_HEADS, DA_HDIM))
        vs_l.append(v_s.reshape(ns, t_new, DA_HEADS, 2 * DA_HDIM))
        hs_l.append(hstate)

    k_prompt = jnp.transpose(kt_all.reshape(depth, nb, 2 * DA_HEADS, DA_HDIM, seq), (0, 1, 4, 2, 3))
    v_prompt = v_all.reshape(depth, nb, seq, DA_HEADS, 2 * DA_HDIM)
    return (xp.reshape(nb, seq, d_model), xs.reshape(ns, t_new, d_model), k_prompt, v_prompt,
            jnp.stack(hp_l), jnp.stack(ks_l), jnp.stack(vs_l), jnp.stack(hs_l))
```

```python
import functools
import math

import numpy as np
import jax
import jax.numpy as jnp
from jax import lax
from jax.experimental import pallas as pl
from jax.experimental.pallas import tpu as pltpu

F32 = jnp.float32
BF16 = jnp.bfloat16

DA_HEADS = 4
DA_HDIM = 64
DA_WIDTH = 2 * DA_HEADS * DA_HDIM
HG_HEADS = 4
HG_DIM = 128
HG_WIDTH = HG_HEADS * HG_DIM
N_GROUPS = 7
ROPE_THETA = 500000.0
ROT_DIM = DA_HDIM // 4
ROT_HALF = ROT_DIM // 2
NORM_EPS = 1e-6
SUBLN_EPS = 1e-5
MASK_VALUE = -1e30
QK_SCALE_LOG2 = DA_HDIM ** -0.5 * math.log2(math.e)
LANES = 128
VMEM_LIMIT = 56 * 1024 * 1024

HG_CHUNK = 64
HG_MATMUL_LEVELS = 3
HG_SEQS_PER_STEP = 2
HG_UNROLL = 4
HG_CHUNK_SAMPLE = 16
ATTN_TQ = 2048
ATTN_TK = 256
ATTN_GROUPS = (8, 4, 2, 1)
PAGES_PER_STEP = 32
SAMPLE_TPAD = 8


def _cparams(sem):
    return pltpu.CompilerParams(dimension_semantics=sem, vmem_limit_bytes=VMEM_LIMIT)


def _dot(a, b):
    return jnp.dot(a, b, preferred_element_type=F32)


def _dot_nt(a, b):
    return lax.dot_general(a, b, (((1,), (1,)), ((), ())), preferred_element_type=F32)


def _dot_tn(a, b):
    return lax.dot_general(a, b, (((0,), (0,)), ((), ())), preferred_element_type=F32)


def _rms(x, eps):
    return x * lax.rsqrt(jnp.mean(x * x, axis=-1, keepdims=True) + eps)


def _sigmoid(x):
    return 1.0 / (1.0 + jnp.exp(-x))


def _rope_rows(z, c_ref, s1_ref, s2_ref):
    c, s1, s2 = c_ref[...], s1_ref[...], s2_ref[...]
    outs = []
    for j in range(DA_WIDTH // LANES):
        zc = z[:, j * LANES:(j + 1) * LANES]
        up = pltpu.roll(zc, LANES - ROT_HALF, 1)
        dn = pltpu.roll(zc, ROT_HALF, 1)
        outs.append(zc * c + up * s1 + dn * s2)
    return jnp.concatenate(outs, axis=1)


def _proj_in_kernel(*refs, layer, k_transposed):
    if k_transposed:
        (x_ref, g_ref, w_ref, wkt_ref, c_ref, s1_ref, s2_ref, ct_ref, st_ref, lbl_ref,
         _kt_alias, _v_alias,
         q_ref, kt_ref, ktb_ref, v_ref, vb_ref, hq_ref, hk_ref, lg_ref, hi_ref, hg_ref) = refs
    else:
        (x_ref, g_ref, w_ref, c_ref, s1_ref, s2_ref, lbl_ref,
         q_ref, k_ref, kb_ref, v_ref, vb_ref, hq_ref, hk_ref, lg_ref, hi_ref, hg_ref) = refs
    xb = (_rms(x_ref[...], NORM_EPS) * g_ref[...]).astype(BF16)

    def cols(g):
        return _dot(xb, w_ref[:, g * DA_WIDTH:(g + 1) * DA_WIDTH])

    q_ref[...] = (_rope_rows(cols(0), c_ref, s1_ref, s2_ref) * QK_SCALE_LOG2).astype(BF16)
    if k_transposed:
        zk = _dot_nt(wkt_ref[...], xb)
        kt_ref[...] = zk
        ktb_ref[...] = zk.astype(BF16)
        ct, st = ct_ref[...], st_ref[...]
        for j in range(2 * DA_HEADS):
            r = j * DA_HDIM
            x1 = zk[r:r + ROT_HALF]
            x2 = zk[r + ROT_HALF:r + ROT_DIM]
            o1 = x1 * ct - x2 * st
            o2 = x2 * ct + x1 * st
            kt_ref[r:r + ROT_HALF, :] = o1
            kt_ref[r + ROT_HALF:r + ROT_DIM, :] = o2
            ktb_ref[r:r + ROT_DIM, :] = jnp.concatenate([o1, o2], axis=0).astype(BF16)
    else:
        k = _rope_rows(cols(1), c_ref, s1_ref, s2_ref)
        k_ref[...] = k
        kb_ref[...] = k.astype(BF16)
    v = cols(2)
    if k_transposed:
        for h in range(DA_HEADS):
            v_ref[:, h, :] = v[:, h * LANES:(h + 1) * LANES]
    else:
        v_ref[...] = v
    vb_ref[...] = v.astype(BF16)
    zq = cols(3)
    hq_ref[...] = zq * _sigmoid(zq)
    lbl = lbl_ref[...]
    e = jnp.exp(lbl - jnp.max(lbl, axis=0, keepdims=True))
    sm = e / jnp.sum(e, axis=0, keepdims=True)
    lb = jnp.zeros_like(sm[0:1])
    for l in range(1, layer + 1):
        lb = lb + sm[l:l + 1]
    hk = (1.0 - lb) * _sigmoid(-cols(4))
    hk_ref[...] = hk
    lg_ref[...] = jnp.log1p(-hk)
    hi_ref[...] = cols(5)
    hg_ref[...] = cols(6)


def _rope_tables(pos):
    inv = ROPE_THETA ** (-(jnp.arange(ROT_HALF, dtype=F32) * 2.0) / ROT_DIM)
    ang = pos.astype(F32)[:, None] * inv[None, :]
    cos, sin = jnp.cos(ang), jnp.sin(ang)
    d = np.arange(LANES) % DA_HDIM
    reps = LANES // ROT_HALF
    cosl, sinl = jnp.tile(cos, (1, reps)), jnp.tile(sin, (1, reps))
    c = jnp.where((d < ROT_DIM)[None, :], cosl, 1.0)
    s1 = jnp.where((d < ROT_HALF)[None, :], -sinl, 0.0)
    s2 = jnp.where(((d >= ROT_HALF) & (d < ROT_DIM))[None, :], sinl, 0.0)
    return c, s1, s2, cos.T, sin.T


def _proj_in(x2d, g_row, w_bf, wkt_bf, tables, lbl, layer, tm, period, kt_prev=None, v_prev=None,
             depth=None, k_transposed=True):
    n, d = x2d.shape
    npt = period // tm
    nb = n // period
    c, s1, s2, ct, st = tables
    row = lambda i: (i, 0)
    const = lambda i: (0, 0)
    tab = lambda i: (i % npt, 0)
    in_specs = [pl.BlockSpec((tm, d), row), pl.BlockSpec((1, d), const),
                pl.BlockSpec(w_bf.shape, const)]
    args = [x2d, g_row, w_bf]
    if k_transposed:
        in_specs.append(pl.BlockSpec(wkt_bf.shape, const))
        args.append(wkt_bf)
    in_specs += [pl.BlockSpec((tm, LANES), tab)] * 3
    args += [c, s1, s2]
    if k_transposed:
        in_specs += [pl.BlockSpec((ROT_HALF, tm), lambda i: (0, i % npt))] * 2
        args += [ct, st]
    in_specs.append(pl.BlockSpec(lbl.shape, const))
    args.append(lbl)
    act = jax.ShapeDtypeStruct((n, DA_WIDTH), F32)
    act_b = jax.ShapeDtypeStruct((n, DA_WIDTH), BF16)
    act_spec = pl.BlockSpec((tm, DA_WIDTH), row)
    aliases = {}
    if k_transposed:
        kt_shape = jax.ShapeDtypeStruct((depth, nb, DA_WIDTH, period), F32)
        v_shape = jax.ShapeDtypeStruct((depth, n, DA_HEADS, LANES), F32)
        if kt_prev is not None:
            aliases = {len(args): 1, len(args) + 1: 3}
            in_specs += [pl.BlockSpec(memory_space=pl.ANY)] * 2
            args += [kt_prev, v_prev]
        else:
            in_specs += [pl.BlockSpec((1, d), const)] * 2
            args += [g_row, g_row]
        out_shape = [act_b, kt_shape, jax.ShapeDtypeStruct((nb, DA_WIDTH, period), BF16),
                     v_shape, act_b] + [act] * 5
        out_specs = [act_spec,
                     pl.BlockSpec((None, None, DA_WIDTH, tm), lambda i: (layer, i // npt, 0, i % npt)),
                     pl.BlockSpec((None, DA_WIDTH, tm), lambda i: (i // npt, 0, i % npt)),
                     pl.BlockSpec((None, tm, DA_HEADS, LANES), lambda i: (layer, i, 0, 0)),
                     act_spec] + [act_spec] * 5
    else:
        out_shape = [act_b, act, act_b, act, act_b] + [act] * 5
        out_specs = [act_spec] * 10
    return pl.pallas_call(
        functools.partial(_proj_in_kernel, layer=layer, k_transposed=k_transposed),
        out_shape=out_shape, grid=(n // tm,), in_specs=in_specs, out_specs=out_specs,
        input_output_aliases=aliases, compiler_params=_cparams(("arbitrary",)),
        name="proj_in_prompt" if k_transposed else "proj_in_sample")(*args)


def _lam_value(lq1, lk1, lq2, lk2, lam_init):
    a = jnp.sum(lq1[...] * lk1[...], axis=-1, keepdims=True)
    b = jnp.sum(lq2[...] * lk2[...], axis=-1, keepdims=True)
    return jnp.exp(a) - jnp.exp(b) + lam_init


def _attn_prompt_kernel(q_ref, kt_ref, v_ref, lq1, lk1, lq2, lk2, o_ref, qq_ref, m_ref, acc_ref,
                        *, tq, tk, lam_init):
    qi = pl.program_id(2)
    nr = tq // tk
    q = q_ref[...]
    lane = lax.broadcasted_iota(jnp.int32, (tk, LANES), 1)
    for r in range(nr):
        qr = q[r * tk:(r + 1) * tk]
        zero = jnp.zeros_like(qr)
        qq_ref[r, 0:tk, :] = jnp.where(lane < DA_HDIM, qr, zero)
        qq_ref[r, tk:2 * tk, :] = jnp.where(lane >= DA_HDIM, qr, zero)
    m_ref[...] = jnp.full(m_ref.shape, MASK_VALUE, F32)
    acc_ref[...] = jnp.zeros(acc_ref.shape, F32)
    ones = jnp.ones((tk, LANES), BF16)

    def block(kb, full, diag):
        off = pl.multiple_of(kb * tk, tk)
        kt = kt_ref[:, pl.ds(off, tk)]
        vx = jnp.concatenate([v_ref[pl.ds(off, tk), :], ones], axis=1)
        for r in list(full) + ([diag] if diag is not None else []):
            s = _dot(qq_ref[r], kt)
            if r == diag:
                row = lax.broadcasted_iota(jnp.int32, s.shape, 0) % tk
                col = lax.broadcasted_iota(jnp.int32, s.shape, 1)
                s = jnp.where(col <= row, s, MASK_VALUE)
            m_prev = m_ref[r]
            m_next = jnp.maximum(m_prev, jnp.max(s, axis=1, keepdims=True))
            alpha = jnp.exp2(m_prev - m_next)
            p = jnp.exp2(s - jnp.concatenate([m_next] * (tk // LANES), axis=1)).astype(BF16)
            acc_ref[r] = acc_ref[r] * jnp.concatenate([alpha, alpha], axis=1) + _dot(p, vx)
            m_ref[r] = m_next

    everyone = range(nr)
    nfull = qi * nr

    def run(base, count):
        for u in range(count):
            block(base + u, everyone, None)

    sizes = [g for g in ATTN_GROUPS if g >= nr] or [nr]
    top = sizes[0]
    ntop = nfull // top

    def top_group(gi, carry):
        run(gi * top, top)
        return carry

    lax.fori_loop(0, ntop, top_group, 0)
    done = ntop * top
    for size in sizes[1:]:
        take = ((nfull - done) // size) % 2 if size > 1 else (nfull - done)
        base = done

        @pl.when(take > 0)
        def _(base=base, size=size):
            run(base, size)

        done = done + take * size
    for r in range(nr):
        block(nfull + r, range(r + 1, nr), r)
    lam = _lam_value(lq1, lk1, lq2, lk2, lam_init)
    for r in range(nr):
        acc = acc_ref[r]
        o1 = acc[0:tk, 0:LANES] / acc[0:tk, LANES:]
        o2 = acc[tk:, 0:LANES] / acc[tk:, LANES:]
        o_ref[r * tk:(r + 1) * tk, :] = o1 - lam * o2


def _attn_prompt(q_bf, kt_bf, v_bf, lams, lam_init, nb, seq):
    tk = min(ATTN_TK, seq)
    tq = min(ATTN_TQ, seq)
    assert tq % tk == 0 and ATTN_GROUPS[-1] == 1
    assert all(a == 2 * b for a, b in zip(ATTN_GROUPS, ATTN_GROUPS[1:]))
    nq, nr = seq // tq, tq // tk
    lam_spec = pl.BlockSpec((1, DA_HDIM), lambda b, h, i: (0, 0))
    return pl.pallas_call(
        functools.partial(_attn_prompt_kernel, tq=tq, tk=tk, lam_init=lam_init),
        out_shape=jax.ShapeDtypeStruct(q_bf.shape, F32),
        grid=(nb, DA_HEADS, nq),
        in_specs=[pl.BlockSpec((tq, LANES), lambda b, h, i: (b * nq + i, h)),
                  pl.BlockSpec((None, LANES, seq), lambda b, h, i: (b, h, 0)),
                  pl.BlockSpec((seq, LANES), lambda b, h, i: (b, h))] + [lam_spec] * 4,
        out_specs=pl.BlockSpec((tq, LANES), lambda b, h, i: (b * nq + i, h)),
        scratch_shapes=[pltpu.VMEM((nr, 2 * tk, LANES), BF16), pltpu.VMEM((nr, 2 * tk, LANES), F32),
                        pltpu.VMEM((nr, 2 * tk, 2 * LANES), F32)],
        compiler_params=_cparams(("arbitrary",) * 3), name="attn_prompt")(q_bf, kt_bf, v_bf, *lams)


def _hgrn_consts(c):
    nl = int(math.log2(c))
    t = np.arange(c)
    lv = np.zeros((nl, c, c), np.float32)
    msk = np.zeros((nl + 1, c, c), np.float32)
    for l in range(nl):
        b = 1 << l
        mid = (t // (2 * b)) * 2 * b + b
        upper = t >= mid
        for i in range(c):
            if upper[i]:
                lv[l, i, mid[i]:i + 1] = 1.0
            else:
                lv[l, i, i + 1:mid[i]] = 1.0
        same = (t[:, None] // (2 * b)) == (t[None, :] // (2 * b))
        msk[l] = same & upper[:, None] & (~upper)[None, :]
    msk[nl] = np.eye(c)
    tri = np.tril(np.ones((c, c), np.float32))
    nmat = min(nl, HG_MATMUL_LEVELS)
    d = np.concatenate([lv[:nmat].reshape(nmat * c, c), tri], axis=0)
    return jnp.asarray(d, BF16), jnp.asarray(msk, F32), nl, nmat


def _hgrn_kernel(*refs, c, nl, nmat, nc, has_s0):
    if has_s0:
        hq_ref, hk_ref, hv_ref, lg_ref, d_ref, msk_ref, s0_ref, o_ref, sout_ref, st_ref = refs
    else:
        hq_ref, hk_ref, hv_ref, lg_ref, d_ref, msk_ref, o_ref, sout_ref, st_ref = refs
    jb = pl.program_id(1)
    heads = range(hq_ref.shape[0] * HG_HEADS)

    @pl.when(jb == 0)
    def _():
        for h in heads:
            st_ref[h] = (s0_ref[h // HG_HEADS, h % HG_HEADS].T if has_s0
                         else jnp.zeros((HG_DIM, HG_DIM), F32))

    def chunk(ci, carry):
        r0 = pl.multiple_of(ci * c, c)
        sls = [(h // HG_HEADS, pl.ds(r0, c),
                slice((h % HG_HEADS) * HG_DIM, (h % HG_HEADS + 1) * HG_DIM)) for h in heads]
        q = [hq_ref[sl] for sl in sls]
        k = [hk_ref[sl] for sl in sls]
        vb = [hv_ref[sl].astype(BF16) for sl in sls]
        st = [st_ref[h] for h in heads]
        e = []
        for sl in sls:
            g = lg_ref[sl]
            g_hi = g.astype(BF16)
            g_lo = (g - g_hi.astype(F32)).astype(BF16)
            e2 = _dot(d_ref[...], jnp.concatenate([g_hi, g_lo], axis=1))
            e.append(e2[:, 0:HG_DIM] + e2[:, HG_DIM:])
        gcum = [e[h][nmat * c:(nmat + 1) * c] for h in heads]

        def level_exponent(h, l):
            if l < nmat:
                return e[h][l * c:(l + 1) * c]
            b = 1 << l
            rows = [jnp.broadcast_to(gcum[h][x + b - 1:x + b], (2 * b, HG_DIM)) for x in range(0, c, 2 * b)]
            diff = gcum[h] - (rows[0] if len(rows) == 1 else jnp.concatenate(rows, axis=0))
            return jnp.minimum(diff, -diff)

        a = [_dot_nt(q[h].astype(BF16), k[h].astype(BF16)) * msk_ref[nl] for h in heads]
        for l in range(nl):
            for h in heads:
                w = jnp.exp(level_exponent(h, l))
                a[h] = a[h] + _dot_nt((q[h] * w).astype(BF16), (k[h] * w).astype(BF16)) * msk_ref[l]
        for h in heads:
            glast = gcum[h][c - 1:c, :]
            o_ref[sls[h]] = (_dot(a[h].astype(BF16), vb[h])
                             + _dot_nt((q[h] * jnp.exp(gcum[h])).astype(BF16), st[h].astype(BF16)))
            khat = (k[h] * jnp.exp(glast - gcum[h])).astype(BF16)
            st_ref[h] = st[h] * jnp.exp(glast) + _dot_tn(vb[h], khat)
        return carry

    lax.fori_loop(0, nc, chunk, 0, unroll=min(nc, HG_UNROLL))

    @pl.when(jb == pl.num_programs(1) - 1)
    def _():
        for h in heads:
            sout_ref[h // HG_HEADS, h % HG_HEADS] = st_ref[h].T


def _hgrn(hq, hk, hv, lg, s0, nb, seq, c, tb):
    d, msk, nl, nmat = _hgrn_consts(c)
    nblk = seq // tb
    gb = HG_SEQS_PER_STEP if nb % HG_SEQS_PER_STEP == 0 else 1
    grouped = lambda a: a.reshape(nb // gb, gb, seq, HG_WIDTH)
    row = pl.BlockSpec((None, gb, tb, HG_WIDTH), lambda b, j: (b, 0, j, 0))
    st_spec = pl.BlockSpec((gb, HG_HEADS, HG_DIM, HG_DIM), lambda b, j: (b, 0, 0, 0))
    in_specs = [row] * 4 + [pl.BlockSpec(d.shape, lambda b, j: (0, 0)),
                            pl.BlockSpec(msk.shape, lambda b, j: (0, 0, 0))]
    args = [grouped(hq), grouped(hk), grouped(hv), grouped(lg), d, msk]
    if s0 is not None:
        in_specs.append(st_spec)
        args.append(s0)
    o, state = pl.pallas_call(
        functools.partial(_hgrn_kernel, c=c, nl=nl, nmat=nmat, nc=tb // c, has_s0=s0 is not None),
        out_shape=[jax.ShapeDtypeStruct((nb // gb, gb, seq, HG_WIDTH), F32),
                   jax.ShapeDtypeStruct((nb, HG_HEADS, HG_DIM, HG_DIM), F32)],
        grid=(nb // gb, nblk), in_specs=in_specs, out_specs=[row, st_spec],
        scratch_shapes=[pltpu.VMEM((gb * HG_HEADS, HG_DIM, HG_DIM), F32)],
        compiler_params=_cparams(("arbitrary", "arbitrary")),
        name="hgrn_sample" if s0 is not None else "hgrn_prompt")(*args)
    return o.reshape(hq.shape), state


def _channel_kernel(*refs, lam_init, ff_chunk, last):
    (ao_ref, ho_ref, hg_ref, x_ref, pe_ref, sg_ref, ng_ref, wo_ref, n2_ref, wu_ref, wd_ref,
     pn_ref, wg_ref, wp_ref) = refs[:14]
    parts = []
    for h in range(DA_HEADS):
        a = ao_ref[:, h * LANES:(h + 1) * LANES]
        parts.append(_rms(a, SUBLN_EPS) * sg_ref[...] * (1.0 - lam_init))
    for h in range(HG_HEADS):
        sl = slice(h * HG_DIM, (h + 1) * HG_DIM)
        gate = hg_ref[:, sl]
        parts.append(_rms(ho_ref[:, sl], NORM_EPS) * ng_ref[...] * (gate * _sigmoid(gate)))
    x = x_ref[...] + _dot(jnp.concatenate(parts, axis=1).astype(BF16), wo_ref[...])
    hn = (_rms(x, NORM_EPS) * n2_ref[...]).astype(BF16)
    acc = x
    for c0 in range(0, wu_ref.shape[1], ff_chunk):
        u = jnp.maximum(_dot(hn, wu_ref[:, c0:c0 + ff_chunk]), 0.0)
        acc = acc + _dot((u * u).astype(BF16), wd_ref[c0:c0 + ff_chunk, :])
    gate = _sigmoid(_dot((_rms(acc, NORM_EPS) * pn_ref[...]).astype(BF16), wg_ref[...]))
    y = acc + gate * _dot(pe_ref[...].astype(BF16), wp_ref[...])
    if last:
        fn_ref, o_ref = refs[14:]
        o_ref[...] = _rms(y, NORM_EPS) * fn_ref[...]
    else:
        refs[14][...] = y


def _resident(shape):
    return pl.BlockSpec(shape, lambda i: (0,) * len(shape), pipeline_mode=pl.Buffered(1))


def _channel(ao, ho, hg, x2d, pe, sg_row, ng_row, wo_bf, n2_row, wu_bf, wd_bf, pn_row, wg_bf, wp_bf,
             fn_row, lam_init, layer, tm):
    n, d = x2d.shape
    row = lambda i: (i, 0)
    half = pl.BlockSpec((tm, DA_WIDTH), row)
    args = [ao, ho, hg, x2d, pe, sg_row, ng_row, wo_bf, n2_row, wu_bf, wd_bf, pn_row, wg_bf, wp_bf]
    in_specs = [half, half, half, pl.BlockSpec((tm, d), row),
                pl.BlockSpec((None, tm, pe.shape[2]), lambda i: (layer, i, 0))]
    in_specs += [_resident(a.shape) for a in args[5:]]
    if fn_row is not None:
        in_specs.append(_resident(fn_row.shape))
        args.append(fn_row)
    return pl.pallas_call(
        functools.partial(_channel_kernel, lam_init=lam_init, ff_chunk=d, last=fn_row is not None),
        out_shape=jax.ShapeDtypeStruct((n, d), F32), grid=(n // tm,),
        in_specs=in_specs, out_specs=pl.BlockSpec((tm, d), row),
        compiler_params=_cparams(("arbitrary",)), name="channel_mix")(*args)


def _attn_sample_kernel(pt_ref, qx_ref, ktn_ref, vn_ref, *refs, pp, page, lam_init):
    kt_refs = refs[:pp]
    v_refs = refs[pp:2 * pp]
    lq1, lk1, lq2, lk2, o_ref, m_ref, l_ref, acc_ref = refs[2 * pp:]
    j = pl.program_id(1)
    rows = qx_ref.shape[0]
    qx = qx_ref[...]
    rhead = (lax.broadcasted_iota(jnp.int32, (rows, LANES), 0) // SAMPLE_TPAD) % DA_HEADS

    def update(kts, vs, mask):
        s = jnp.concatenate([_dot(qx, kt[...].astype(BF16)) for kt in kts], axis=1)
        if mask is not None:
            s = jnp.where(mask, s, MASK_VALUE)
        m_prev = m_ref[...]
        m_next = jnp.maximum(m_prev, jnp.max(s, axis=1, keepdims=True))
        alpha = jnp.exp2(m_prev - m_next)
        p = jnp.exp2(s - jnp.concatenate([m_next] * len(kts), axis=1))
        l_ref[...] = l_ref[...] * alpha + jnp.sum(p, axis=1, keepdims=True)
        pb = p.astype(BF16)
        acc = acc_ref[...] * alpha
        for i, v_ref in enumerate(vs):
            pi = pb[:, i * page:(i + 1) * page]
            for h in range(DA_HEADS):
                vh = v_ref[pl.ds(h, page, stride=DA_HEADS), :].astype(BF16)
                acc = acc + jnp.where(rhead == h, _dot(pi, vh), 0.0)
        acc_ref[...] = acc
        m_ref[...] = m_next

    @pl.when(j == 0)
    def _():
        m_ref[...] = jnp.full(m_ref.shape, MASK_VALUE, F32)
        l_ref[...] = jnp.zeros(l_ref.shape, F32)
        acc_ref[...] = jnp.zeros(acc_ref.shape, F32)
        tok = lax.broadcasted_iota(jnp.int32, (rows, page), 0) % SAMPLE_TPAD
        col = lax.broadcasted_iota(jnp.int32, (rows, page), 1)
        update([ktn_ref], [vn_ref], col <= tok)

    update(kt_refs, v_refs, None)

    @pl.when(j == pl.num_programs(1) - 1)
    def _():
        o = acc_ref[...] / l_ref[...]
        lam = _lam_value(lq1, lk1, lq2, lk2, lam_init)
        half = rows // 2
        for h in range(DA_HEADS):
            pos = o[h * SAMPLE_TPAD:(h + 1) * SAMPLE_TPAD]
            neg = o[half + h * SAMPLE_TPAD:half + (h + 1) * SAMPLE_TPAD]
            o_ref[:, h * LANES:(h + 1) * LANES] = pos - lam * neg


def _attn_sample(page_table, qx, ktn, vn, cache_kt, cache_v2, lams, lam_init, layer):
    nbatch, n_pages = page_table.shape
    page = cache_kt.shape[-1]
    pp = PAGES_PER_STEP
    while n_pages % pp:
        pp //= 2
    rows = qx.shape[1]
    pt_flat = page_table.reshape(-1)

    def page_map(i):
        return lambda b, j, pt: (layer, pt[b * n_pages + j * pp + i], 0, 0)

    per_b = lambda b, j, pt: (b, 0, 0)
    lam_spec = pl.BlockSpec((1, DA_HDIM), lambda b, j, pt: (0, 0))
    kt_specs = [pl.BlockSpec((None, None, DA_WIDTH, page), page_map(i)) for i in range(pp)]
    v_specs = [pl.BlockSpec((None, None, DA_HEADS * page, LANES), page_map(i)) for i in range(pp)]
    grid_spec = pltpu.PrefetchScalarGridSpec(
        num_scalar_prefetch=1, grid=(nbatch, n_pages // pp),
        in_specs=[pl.BlockSpec((None, rows, DA_WIDTH), per_b),
                  pl.BlockSpec((None, DA_WIDTH, page), per_b),
                  pl.BlockSpec((None, DA_HEADS * page, LANES), per_b)] + kt_specs + v_specs + [lam_spec] * 4,
        out_specs=pl.BlockSpec((None, SAMPLE_TPAD, DA_WIDTH), per_b),
        scratch_shapes=[pltpu.VMEM((rows, LANES), F32), pltpu.VMEM((rows, LANES), F32),
                        pltpu.VMEM((rows, LANES), F32)])
    return pl.pallas_call(
        functools.partial(_attn_sample_kernel, pp=pp, page=page, lam_init=lam_init),
        out_shape=jax.ShapeDtypeStruct((nbatch, SAMPLE_TPAD, DA_WIDTH), F32),
        grid_spec=grid_spec, compiler_params=_cparams(("arbitrary", "arbitrary")),
        name="attn_sample")(pt_flat, qx, ktn, vn, *([cache_kt] * pp), *([cache_v2] * pp), *lams)


def _row_tile(n, want):
    t = min(n, want)
    while n % t:
        t //= 2
    return t


def kernel(x_prompt, x_sample, cache_k, cache_v, state_h, page_table, p_prompt, p_sample,
           norm1_g, w_in, lam_q1, lam_k1, lam_q2, lam_k2, subln_g, hg_lb_logits, hg_norm_g,
           w_out, norm2_g, w_up, w_down, ple_norm_g, w_ple_gate, w_ple_proj, final_norm_g):
    nb, seq, d_model = x_prompt.shape
    ns, t_new, _ = x_sample.shape
    depth = w_in.shape[0]
    n_pool, page = cache_k.shape[1], cache_k.shape[2]
    past_len = page_table.shape[1] * page
    n_p, n_s = nb * seq, ns * t_new
    assert w_in.shape[2] == N_GROUPS * DA_WIDTH and t_new <= SAMPLE_TPAD <= page
    assert seq % HG_CHUNK == 0 and t_new <= HG_CHUNK_SAMPLE

    tm_p = _row_tile(seq, 512)
    tm_s = n_s
    tb_h = _row_tile(seq, 4 * HG_CHUNK)
    tables_p = _rope_tables(jnp.arange(seq, dtype=jnp.int32))
    tables_s = _rope_tables(jnp.tile(past_len + jnp.arange(t_new, dtype=jnp.int32), ns))

    cache_kt = jnp.transpose(cache_k, (0, 1, 3, 4, 2)).reshape(depth, n_pool, DA_WIDTH, page)
    cache_v2 = cache_v.reshape(depth, n_pool, page * DA_HEADS, LANES)

    lane_map = np.arange(DA_WIDTH) // DA_HDIM
    qmask = np.zeros((2, DA_HEADS, 1, DA_WIDTH), np.float32)
    for sgn in range(2):
        for h in range(DA_HEADS):
            qmask[sgn, h, 0] = lane_map == 2 * h + sgn
    qmask = jnp.asarray(qmask, BF16)

    xp = x_prompt.reshape(n_p, d_model)
    xs = x_sample.reshape(n_s, d_model)
    kt_all = v_all = None
    hp_l, ks_l, vs_l, hs_l = [], [], [], []
    row1 = lambda a: a.reshape(1, -1)
    for i in range(depth):
        lam_init = 0.8 - 0.6 * math.exp(-0.3 * i)
        lams = [row1(lam_q1[i]), row1(lam_k1[i]), row1(lam_q2[i]), row1(lam_k2[i])]
        w_bf = w_in[i].astype(BF16)
        wkt_bf = w_in[i][:, DA_WIDTH:2 * DA_WIDTH].T.astype(BF16)
        wo_bf, wu_bf, wd_bf = w_out[i].astype(BF16), w_up[i].astype(BF16), w_down[i].astype(BF16)
        wg_bf, wp_bf = w_ple_gate[i].astype(BF16), w_ple_proj[i].astype(BF16)
        g1, sg, ng = row1(norm1_g[i]), row1(subln_g[i]), row1(hg_norm_g[i])
        n2, pn = row1(norm2_g[i]), row1(ple_norm_g[i])
        fn = row1(final_norm_g) if i == depth - 1 else None

        (q_bf, kt_all, kt_bf, v_all, v_bf, hq, hk, lg, hi, hg) = _proj_in(
            xp, g1, w_bf, wkt_bf, tables_p, hg_lb_logits, i, tm_p, seq,
            kt_prev=kt_all, v_prev=v_all, depth=depth, k_transposed=True)
        ao = _attn_prompt(q_bf, kt_bf, v_bf, lams, lam_init, nb, seq)
        ho, hstate = _hgrn(hq, hk, hi, lg, None, nb, seq, HG_CHUNK, tb_h)
        xp = _channel(ao, ho, hg, xp, p_prompt.reshape(depth, n_p, -1), sg, ng, wo_bf, n2, wu_bf, wd_bf,
                      pn, wg_bf, wp_bf, fn, lam_init, i, tm_p)
        hp_l.append(hstate)

        (q_bf, k_s, k_bf, v_s, v_bf, hq, hk, lg, hi, hg) = _proj_in(
            xs, g1, w_bf, None, tables_s, hg_lb_logits, i, tm_s, n_s, k_transposed=False)
        qpad = jnp.pad(q_bf.reshape(ns, 1, 1, t_new, DA_WIDTH),
                       ((0, 0), (0, 0), (0, 0), (0, SAMPLE_TPAD - t_new), (0, 0)))
        qx = (qpad * qmask).reshape(ns, 2 * DA_HEADS * SAMPLE_TPAD, DA_WIDTH)
        ktn = jnp.pad(jnp.swapaxes(k_bf.reshape(ns, t_new, DA_WIDTH), 1, 2),
                      ((0, 0), (0, 0), (0, page - t_new)))
        vn = jnp.pad(v_s.reshape(ns, t_new, DA_WIDTH), ((0, 0), (0, page - t_new), (0, 0)))
        vn = vn.reshape(ns, page * DA_HEADS, LANES)
        ao8 = _attn_sample(page_table, qx, ktn, vn, cache_kt, cache_v2, lams, lam_init, i)
        ao = ao8[:, :t_new].reshape(n_s, DA_WIDTH)
        padc = lambda a: jnp.pad(a.reshape(ns, t_new, HG_WIDTH),
                                 ((0, 0), (0, HG_CHUNK_SAMPLE - t_new), (0, 0))).reshape(-1, HG_WIDTH)
        ho16, hstate = _hgrn(padc(hq), padc(hk), padc(hi), padc(lg), state_h[i], ns,
                             HG_CHUNK_SAMPLE, HG_CHUNK_SAMPLE, HG_CHUNK_SAMPLE)
        ho = ho16.reshape(ns, HG_CHUNK_SAMPLE, HG_WIDTH)[:, :t_new].reshape(n_s, HG_WIDTH)
        xs = _channel(ao, ho, hg, xs, p_sample.reshape(depth, n_s, -1), sg, ng, wo_bf, n2, wu_bf, wd_bf,
                      pn, wg_bf, wp_bf, fn, lam_init, i, tm_s)
        ks_l.append(k_s.reshape(ns, t_new, 2 * DA_HEADS, DA_HDIM))
        vs_l.append(v_s.reshape(ns, t_new, DA_HEADS, 2 * DA_HDIM))
        hs_l.append(hstate)

    k_prompt = jnp.transpose(kt_all.reshape(depth, nb, 2 * DA_HEADS, DA_HDIM, seq), (0, 1, 4, 2, 3))
    v_prompt = v_all.reshape(depth, nb, seq, DA_HEADS, 2 * DA_HDIM)
    return (xp.reshape(nb, seq, d_model), xs.reshape(ns, t_new, d_model), k_prompt, v_prompt,
            jnp.stack(hp_l), jnp.stack(ks_l), jnp.stack(vs_l), jnp.stack(hs_l))
```
